```python
import jax, jax.numpy as jnp
from jax import lax
import numpy as np

D_MODEL = 1024
BATCH = 32
SEQ = 2048
DEPTH = 4

MEM_LEN = 256
D_CONV = D_MODEL
CONV_KERNEL = 31
D_SHORT = D_MODEL
SHORT_KERNEL = 3
D_POOL = D_MODEL
POOL_WINDOWS = (2, 4, 8, 16)
N_POOL_GROUPS = 4
POOL_GROUP = D_POOL // N_POOL_GROUPS
N_BRANCHES = 3
D_IN_PROJ = 2 * D_CONV + 3 * D_SHORT + D_POOL + N_BRANCHES * D_MODEL
N_XHEADS = 4
XHEAD_DIM = D_MODEL // N_XHEADS
N_EXPERTS = 32
TOP_K = 4
D_EXPERT = D_MODEL
SWIGLU_LIMIT = 7.0
SWIGLU_ALPHA = 1.702
EXPERT_BLOCK = 128
DEEPNORM_ALPHA = float((2 * DEPTH) ** 0.25)
DEEPNORM_BETA = float((8 * DEPTH) ** -0.25)
LN_EPS = 1e-5

kernel_name = "hybrid_conv_pool_xattn_moe_deepnorm"


def layer_norm(x, g, b):
    xf = x.astype(jnp.float32)
    mu = jnp.mean(xf, axis=-1, keepdims=True)
    var = jnp.mean(jnp.square(xf - mu), axis=-1, keepdims=True)
    y = (xf - mu) * lax.rsqrt(var + LN_EPS)
    return (y * g + b).astype(x.dtype)


def causal_depthwise_conv(u, w):
    k_width, ch = w.shape
    return lax.conv_general_dilated(
        u, w[:, None, :].astype(u.dtype), window_strides=(1,), padding=[(k_width - 1, 0)],
        dimension_numbers=("NWC", "WIO", "NWC"), feature_group_count=ch)


def causal_multiscale_pool(u):
    bsz, t_len, _ = u.shape
    uf = u.astype(jnp.float32).reshape(bsz, t_len, N_POOL_GROUPS, POOL_GROUP)
    cs = jnp.cumsum(uf, axis=1)
    pos = jnp.arange(t_len)
    outs = []
    for gi, win in enumerate(POOL_WINDOWS):
        c = cs[:, :, gi]
        prev = jnp.pad(c, ((0, 0), (win, 0), (0, 0)))[:, :t_len]
        cnt = jnp.minimum(pos + 1, win).astype(jnp.float32)
        outs.append((c - prev) / cnt[None, :, None])
    pooled = jnp.stack(outs, axis=2)
    return (pooled - uf).astype(u.dtype)


def mixer_sublayer(x, w_in, b_in, conv_a_w, conv_a_b, ln_a_g, ln_a_b, w_a_out, b_a_out,
                   conv_b_w, w_b_out, pool_w, pool_scale, w_mix_out):
    bsz, t_len, _ = x.shape
    p = x @ w_in + b_in
    o1 = 2 * D_CONV
    o2 = o1 + D_SHORT
    o3 = o2 + D_SHORT
    o4 = o3 + D_SHORT
    o5 = o4 + D_POOL
    a_in, gate_b, gate_c, v_bc, pool_in, gates = (
        p[..., :o1], p[..., o1:o2], p[..., o2:o3], p[..., o3:o4], p[..., o4:o5], p[..., o5:])
    a = a_in[..., :D_CONV] * jax.nn.sigmoid(a_in[..., D_CONV:])
    a = causal_depthwise_conv(a, conv_a_w) + conv_a_b
    a = jax.nn.silu(layer_norm(a, ln_a_g, ln_a_b))
    y_a = a @ w_a_out + b_a_out
    u = causal_depthwise_conv(gate_c * v_bc, conv_b_w)
    y_b = (gate_b * u) @ w_b_out
    pooled = causal_multiscale_pool(pool_in)
    y_c = jnp.einsum("btgc,gcd->btgd", pooled, pool_w).reshape(bsz, t_len, D_POOL) * pool_scale
    g_a = gates[..., :D_MODEL]
    g_b = gates[..., D_MODEL:2 * D_MODEL]
    g_c = gates[..., 2 * D_MODEL:]
    m = jax.nn.sigmoid(g_a) * y_a + jax.nn.sigmoid(g_b) * y_b + jax.nn.sigmoid(g_c) * y_c
    return m @ w_mix_out


def memory_cross_attention(x, mem_n, w_q, w_k, w_v, w_o):
    bsz, t_len, d = x.shape
    m_len = mem_n.shape[1]
    q = (x @ w_q).reshape(bsz, t_len, N_XHEADS, XHEAD_DIM)
    k = (mem_n @ w_k).reshape(bsz, m_len, N_XHEADS, XHEAD_DIM)
    v = (mem_n @ w_v).reshape(bsz, m_len, N_XHEADS, XHEAD_DIM)
    s = jnp.einsum("bshk,bmhk->bhsm", q.astype(jnp.float32), k.astype(jnp.float32)) * (XHEAD_DIM ** -0.5)
    prob = jax.nn.softmax(s, axis=-1).astype(v.dtype)
    o = jnp.einsum("bhsm,bmhk->bshk", prob, v).reshape(bsz, t_len, d)
    return o @ w_o


def clamped_swiglu(gu):
    gate = jnp.minimum(gu[..., :D_EXPERT], SWIGLU_LIMIT)
    up = jnp.clip(gu[..., D_EXPERT:], -SWIGLU_LIMIT, SWIGLU_LIMIT)
    return (up + 1.0) * (gate * jax.nn.sigmoid(gate * SWIGLU_ALPHA))


def moe_sublayer(h, router_w, router_b, w_gu, b_gu, w_down, b_down):
    bsz, t_len, d = h.shape
    n_tok = bsz * t_len
    hf = h.reshape(n_tok, d)
    logits = (hf @ router_w).astype(jnp.float32) + router_b.astype(jnp.float32)
    top_logits, top_idx = lax.top_k(logits, TOP_K)
    gates = jax.nn.softmax(top_logits, axis=-1)
    n_assign = n_tok * TOP_K
    flat_e = top_idx.reshape(n_assign).astype(jnp.int32)
    flat_tok = jnp.arange(n_assign, dtype=jnp.int32) // TOP_K
    flat_gate = gates.reshape(n_assign)
    order = jnp.argsort(flat_e)
    e_sorted = flat_e[order]
    counts = jnp.bincount(flat_e, length=N_EXPERTS).astype(jnp.int32)
    padded = ((counts + EXPERT_BLOCK - 1) // EXPERT_BLOCK) * EXPERT_BLOCK
    pad_end = jnp.cumsum(padded)
    pad_start = pad_end - padded
    grp_start = jnp.cumsum(counts) - counts
    dest = pad_start[e_sorted] + (jnp.arange(n_assign, dtype=jnp.int32) - grp_start[e_sorted])
    n_blocks = (n_assign + N_EXPERTS * (EXPERT_BLOCK - 1) + EXPERT_BLOCK - 1) // EXPERT_BLOCK
    n_rows = n_blocks * EXPERT_BLOCK
    row_tok = jnp.full((n_rows,), n_tok, jnp.int32).at[dest].set(flat_tok[order])
    row_gate = jnp.zeros((n_rows,), jnp.float32).at[dest].set(flat_gate[order])
    block_start = jnp.arange(n_blocks, dtype=jnp.int32) * EXPERT_BLOCK
    block_e = jnp.minimum(jnp.searchsorted(pad_end, block_start, side="right"), N_EXPERTS - 1)
    h_pad = jnp.concatenate([hf, jnp.zeros((1, d), hf.dtype)], axis=0)

    def step(acc, blk):
        tok, g, e = blk
        xb = h_pad[tok]
        act = clamped_swiglu(xb @ w_gu[e] + b_gu[e])
        yb = act @ w_down[e] + b_down[e]
        return acc.at[tok].add(yb * g[:, None].astype(yb.dtype)), None

    acc0 = jnp.zeros((n_tok + 1, d), h.dtype)
    acc, _ = lax.scan(step, acc0, (row_tok.reshape(n_blocks, EXPERT_BLOCK),
                                   row_gate.reshape(n_blocks, EXPERT_BLOCK), block_e))
    return acc[:n_tok].reshape(bsz, t_len, d)


def setup_inputs(seed: int = 0) -> dict:
    key = jax.random.key(seed)
    ks = jax.random.split(key, 40)
    L = DEPTH
    f32 = jnp.float32

    def nrm(k, shape, scale):
        return jax.random.normal(k, shape, f32) * scale

    def gain(k, shape):
        return 1.0 + 0.05 * jax.random.normal(k, shape, f32)

    def bias(k, shape, scale=0.02):
        return scale * jax.random.normal(k, shape, f32)

    return {
        "x": nrm(ks[0], (BATCH, SEQ, D_MODEL), 1.0),
        "mem": nrm(ks[1], (BATCH, MEM_LEN, D_MODEL), 1.0),
        "mem_ln_g": gain(ks[2], (D_MODEL,)),
        "mem_ln_b": bias(ks[3], (D_MODEL,)),
        "w_in": nrm(ks[4], (L, D_MODEL, D_IN_PROJ), D_MODEL ** -0.5),
        "b_in": bias(ks[5], (L, D_IN_PROJ)),
        "conv_a_w": nrm(ks[6], (L, CONV_KERNEL, D_CONV), CONV_KERNEL ** -0.5),
        "conv_a_b": bias(ks[7], (L, D_CONV)),
        "ln_a_g": gain(ks[8], (L, D_CONV)),
        "ln_a_b": bias(ks[9], (L, D_CONV)),
        "w_a_out": nrm(ks[10], (L, D_CONV, D_MODEL), D_CONV ** -0.5),
        "b_a_out": bias(ks[11], (L, D_MODEL)),
        "conv_b_w": nrm(ks[12], (L, SHORT_KERNEL, D_SHORT), SHORT_KERNEL ** -0.5),
        "w_b_out": nrm(ks[13], (L, D_SHORT, D_MODEL), D_SHORT ** -0.5),
        "pool_w": nrm(ks[14], (L, N_POOL_GROUPS, POOL_GROUP, POOL_GROUP), POOL_GROUP ** -0.5),
        "pool_scale": gain(ks[15], (L, D_POOL)),
        "w_mix_out": nrm(ks[16], (L, D_MODEL, D_MODEL), DEEPNORM_BETA * D_MODEL ** -0.5),
        "ln1_g": gain(ks[17], (L, D_MODEL)),
        "ln1_b": bias(ks[18], (L, D_MODEL)),
        "w_xq": nrm(ks[19], (L, D_MODEL, D_MODEL), D_MODEL ** -0.5),
        "w_xk": nrm(ks[20], (L, D_MODEL, D_MODEL), D_MODEL ** -0.5),
        "w_xv": nrm(ks[21], (L, D_MODEL, D_MODEL), DEEPNORM_BETA * D_MODEL ** -0.5),
        "w_xo": nrm(ks[22], (L, D_MODEL, D_MODEL), DEEPNORM_BETA * D_MODEL ** -0.5),
        "ln2_g": gain(ks[23], (L, D_MODEL)),
        "ln2_b": bias(ks[24], (L, D_MODEL)),
        "router_w": nrm(ks[25], (L, D_MODEL, N_EXPERTS), D_MODEL ** -0.5),
        "router_b": bias(ks[26], (L, N_EXPERTS), 0.01),
        "w_gu": nrm(ks[27], (L, N_EXPERTS, D_MODEL, 2 * D_EXPERT), D_MODEL ** -0.5),
        "b_gu": bias(ks[28], (L, N_EXPERTS, 2 * D_EXPERT), 0.01),
        "w_down": nrm(ks[29], (L, N_EXPERTS, D_EXPERT, D_MODEL), DEEPNORM_BETA * D_EXPERT ** -0.5),
        "b_down": bias(ks[30], (L, N_EXPERTS, D_MODEL), 0.01),
        "ln3_g": gain(ks[31], (L, D_MODEL)),
        "ln3_b": bias(ks[32], (L, D_MODEL)),
    }


def reference(x, mem, mem_ln_g, mem_ln_b, w_in, b_in, conv_a_w, conv_a_b, ln_a_g, ln_a_b,
              w_a_out, b_a_out, conv_b_w, w_b_out, pool_w, pool_scale, w_mix_out, ln1_g, ln1_b,
              w_xq, w_xk, w_xv, w_xo, ln2_g, ln2_b, router_w, router_b, w_gu, b_gu, w_down,
              b_down, ln3_g, ln3_b):
    mem_n = layer_norm(mem, mem_ln_g, mem_ln_b)
    for l in range(DEPTH):
        mix = mixer_sublayer(x, w_in[l], b_in[l], conv_a_w[l], conv_a_b[l], ln_a_g[l], ln_a_b[l],
                             w_a_out[l], b_a_out[l], conv_b_w[l], w_b_out[l], pool_w[l],
                             pool_scale[l], w_mix_out[l])
        x = layer_norm(DEEPNORM_ALPHA * x + mix, ln1_g[l], ln1_b[l])
        xa = memory_cross_attention(x, mem_n, w_xq[l], w_xk[l], w_xv[l], w_xo[l])
        x = layer_norm(DEEPNORM_ALPHA * x + xa, ln2_g[l], ln2_b[l])
        ff = moe_sublayer(x, router_w[l], router_b[l], w_gu[l], b_gu[l], w_down[l], b_down[l])
        x = layer_norm(DEEPNORM_ALPHA * x + ff, ln3_g[l], ln3_b[l])
    return x
```

```python
import functools

import jax
import jax.numpy as jnp
from jax import lax
from jax.experimental import pallas as pl
from jax.experimental.pallas import tpu as pltpu

N_XHEADS = 4
TOP_K = 4
POOL_WINDOWS = (2, 4, 8, 16)
SWIGLU_LIMIT = 7.0
SWIGLU_ALPHA = 1.702
LN_EPS = 1e-5

SUBLANES = 8
LANES = 128
VMEM_LIMIT_BYTES = 56 * 1024 * 1024

BF16 = jnp.bfloat16
F32 = jnp.float32


def _round_up(n, m):
    return (n + m - 1) // m * m


def _layer_norm(v, g, b):
    mu = jnp.mean(v, axis=-1, keepdims=True)
    c = v - mu
    var = jnp.mean(c * c, axis=-1, keepdims=True)
    return c * lax.rsqrt(var + LN_EPS) * g + b


def _dot(a, b):
    return jnp.dot(a, b, preferred_element_type=F32)


def _resident(block_shape, index_map):
    return pl.BlockSpec(block_shape, index_map, pipeline_mode=pl.Buffered(1))


def _kv_kernel(mem_ref, g_ref, b_ref, wk_ref, wv_ref, k_ref, v_ref):
    mn = _layer_norm(mem_ref[...], g_ref[...], b_ref[...]).astype(BF16)
    k_ref[...] = _dot(mn, wk_ref[...]).astype(BF16)
    v_ref[...] = _dot(mn, wv_ref[...]).astype(BF16)


def _kv_call(mem, g, b, wk, wv):
    bsz, m_len, d = mem.shape
    depth = wk.shape[0]
    out = jax.ShapeDtypeStruct((depth, bsz, m_len, d), BF16)
    return pl.pallas_call(
        _kv_kernel,
        out_shape=(out, out),
        grid=(depth, bsz),
        in_specs=[
            pl.BlockSpec((None, m_len, d), lambda l, i: (i, 0, 0)),
            pl.BlockSpec((1, d), lambda l, i: (0, 0)),
            pl.BlockSpec((1, d), lambda l, i: (0, 0)),
            pl.BlockSpec((None, d, d), lambda l, i: (l, 0, 0)),
            pl.BlockSpec((None, d, d), lambda l, i: (l, 0, 0)),
        ],
        out_specs=(
            pl.BlockSpec((None, None, m_len, d), lambda l, i: (l, i, 0, 0)),
            pl.BlockSpec((None, None, m_len, d), lambda l, i: (l, i, 0, 0)),
        ),
        compiler_params=pltpu.CompilerParams(
            dimension_semantics=("arbitrary", "arbitrary"), vmem_limit_bytes=VMEM_LIMIT_BYTES),
        name="mem_kv",
    )(mem, g, b, wk, wv)


def _causal_taps(buf, w_ref, halo, tb, lane_chunk, row_chunk, out_ref):
    k_width = w_ref.shape[0]
    d = buf.shape[1]
    first = halo - (k_width - 1)
    for c0 in range(0, d, lane_chunk):
        for r0 in range(0, tb, row_chunk):
            acc = None
            for k in range(k_width):
                term = w_ref[k:k + 1, c0:c0 + lane_chunk] * buf[first + k + r0:first + k + r0 + row_chunk,
                                                                c0:c0 + lane_chunk]
                acc = term if acc is None else acc + term
            out_ref[r0:r0 + row_chunk, c0:c0 + lane_chunk] = acc


def _mixer_kernel(x_ref, w_in_ref, b_in_ref, caw_ref, cab_ref, lag_ref, lab_ref, wao_ref, bao_ref,
                  cbw_ref, wbo_ref, pw_ref, ps_ref, wmo_ref, l1g_ref, l1b_ref, o_ref,
                  abuf, bbuf, cbuf, tmp, *, alpha, halo_a, halo_b, halo_c):
    tb, d = x_ref.shape
    t = pl.program_id(1)

    @pl.when(t == 0)
    def _():
        abuf[0:halo_a, :] = jnp.zeros((halo_a, d), F32)
        bbuf[0:halo_b, :] = jnp.zeros((halo_b, d), F32)
        cbuf[0:halo_c, :] = jnp.zeros((halo_c, d), F32)

    x = x_ref[...]
    xb = x.astype(BF16)

    def proj(j):
        return _dot(xb, w_in_ref[:, j * d:(j + 1) * d]) + b_in_ref[:, j * d:(j + 1) * d]

    abuf[halo_a:halo_a + tb, :] = proj(0) * jax.nn.sigmoid(proj(1))
    _causal_taps(abuf, caw_ref, halo_a, tb, LANES, 64, tmp)
    a = _layer_norm(tmp[...] + cab_ref[...], lag_ref[...], lab_ref[...])
    a = a * jax.nn.sigmoid(a)
    y_a = _dot(a.astype(BF16), wao_ref[...]) + bao_ref[...]
    m = jax.nn.sigmoid(proj(6)) * y_a
    abuf[0:halo_a, :] = abuf[tb:tb + halo_a, :]

    bbuf[halo_b:halo_b + tb, :] = proj(3) * proj(4)
    _causal_taps(bbuf, cbw_ref, halo_b, tb, LANES, 128, tmp)
    y_b = _dot((proj(2) * tmp[...]).astype(BF16), wbo_ref[...])
    m = m + jax.nn.sigmoid(proj(7)) * y_b
    bbuf[0:halo_b, :] = bbuf[tb:tb + halo_b, :]

    cbuf[halo_c:halo_c + tb, :] = proj(5)
    n_groups = len(POOL_WINDOWS)
    pg = d // n_groups
    pos = t * tb + lax.broadcasted_iota(jnp.int32, (tb, 1), 0)
    y_parts = []
    for gi, win in enumerate(POOL_WINDOWS):
        lo = gi * pg
        u = cbuf[halo_c:halo_c + tb, lo:lo + pg]
        s = u
        for j in range(1, win):
            s = s + cbuf[halo_c - j:halo_c - j + tb, lo:lo + pg]
        cnt = jnp.minimum(pos + 1, win).astype(F32)
        pooled = s / cnt - u
        y_parts.append(_dot(pooled.astype(BF16), pw_ref[gi]))
    y_c = jnp.concatenate(y_parts, axis=-1) * ps_ref[...]
    m = m + jax.nn.sigmoid(proj(8)) * y_c
    cbuf[0:halo_c, :] = cbuf[tb:tb + halo_c, :]

    mix = _dot(m.astype(BF16), wmo_ref[...])
    o_ref[...] = _layer_norm(alpha * x + mix, l1g_ref[...], l1b_ref[...])


def _mixer_call(l, x, w_in, b_in, caw, cab, lag, lab, wao, bao, cbw, wbo, pw, ps, wmo, l1g, l1b, *, alpha, tb):
    bsz, t_len, d = x.shape
    d_in = w_in.shape[2]
    ka, kb = caw.shape[1], cbw.shape[1]
    n_groups, pg = pw.shape[1], pw.shape[2]
    halo_a = _round_up(ka - 1, SUBLANES)
    halo_b = _round_up(kb - 1, SUBLANES)
    halo_c = _round_up(max(POOL_WINDOWS) - 1, SUBLANES)
    assert t_len % tb == 0 and tb % 128 == 0 and tb >= halo_a
    vec = lambda n: pl.BlockSpec((None, 1, n), lambda i, t: (l, 0, 0))
    kern = functools.partial(_mixer_kernel, alpha=alpha, halo_a=halo_a, halo_b=halo_b, halo_c=halo_c)
    return pl.pallas_call(
        kern,
        out_shape=jax.ShapeDtypeStruct((bsz, t_len, d), F32),
        grid=(bsz, t_len // tb),
        in_specs=[
            pl.BlockSpec((None, tb, d), lambda i, t: (i, t, 0)),
            _resident((None, d, d_in), lambda i, t: (l, 0, 0)),
            vec(d_in),
            pl.BlockSpec((None, ka, d), lambda i, t: (l, 0, 0)),
            vec(d), vec(d), vec(d),
            _resident((None, d, d), lambda i, t: (l, 0, 0)),
            vec(d),
            pl.BlockSpec((None, kb, d), lambda i, t: (l, 0, 0)),
            _resident((None, d, d), lambda i, t: (l, 0, 0)),
            _resident((None, n_groups, pg, pg), lambda i, t: (l, 0, 0, 0)),
            vec(d),
            _resident((None, d, d), lambda i, t: (l, 0, 0)),
            vec(d), vec(d),
        ],
        out_specs=pl.BlockSpec((None, tb, d), lambda i, t: (i, t, 0)),
        scratch_shapes=[
            pltpu.VMEM((halo_a + tb, d), F32),
            pltpu.VMEM((halo_b + tb, d), F32),
            pltpu.VMEM((halo_c + tb, d), F32),
            pltpu.VMEM((tb, d), F32),
        ],
        compiler_params=pltpu.CompilerParams(
            dimension_semantics=("arbitrary", "arbitrary"), vmem_limit_bytes=VMEM_LIMIT_BYTES),
        name="mixer",
    )(x, w_in, b_in, caw, cab, lag, lab, wao, bao, cbw, wbo, pw, ps, wmo, l1g, l1b)


def _xattn_kernel(x_ref, k_ref, v_ref, wq_ref, wo_ref, l2g_ref, l2b_ref, rwt_ref, rb_ref,
                  o_ref, ti_ref, gt_ref, rk_ref, cnt_ref, run_ref, *, alpha):
    tb, d = x_ref.shape
    n_exp = rwt_ref.shape[0]
    dh = d // N_XHEADS
    scale = dh ** -0.5
    first_step = jnp.logical_and(pl.program_id(0) == 0, pl.program_id(1) == 0)

    @pl.when(first_step)
    def _():
        run_ref[...] = jnp.zeros(run_ref.shape, F32)

    x = x_ref[...]
    q = _dot(x.astype(BF16), wq_ref[...])
    heads = []
    for h in range(N_XHEADS):
        qh = q[:, h * dh:(h + 1) * dh].astype(BF16)
        kh = k_ref[:, h * dh:(h + 1) * dh]
        vh = v_ref[:, h * dh:(h + 1) * dh]
        s = lax.dot_general(qh, kh, (((1,), (1,)), ((), ())), preferred_element_type=F32) * scale
        p = jnp.exp(s - jnp.max(s, axis=-1, keepdims=True))
        denom = jnp.sum(p, axis=-1, keepdims=True)
        heads.append(_dot(p.astype(BF16), vh) / denom)
    att = _dot(jnp.concatenate(heads, axis=-1).astype(BF16), wo_ref[...])
    x2 = _layer_norm(alpha * x + att, l2g_ref[...], l2b_ref[...])
    o_ref[...] = x2

    logits = lax.dot_general(rwt_ref[...], x2.astype(BF16), (((1,), (1,)), ((), ())),
                             preferred_element_type=F32) + rb_ref[...]
    eidx = lax.broadcasted_iota(jnp.int32, (n_exp, tb), 0).astype(F32)
    work = logits
    tops, args = [], []
    member = jnp.zeros((n_exp, tb), F32)
    for _ in range(TOP_K):
        mx = jnp.max(work, axis=0, keepdims=True)
        am = jnp.min(jnp.where(work == mx, eidx, float(n_exp)), axis=0, keepdims=True)
        hit = eidx == am
        tops.append(mx)
        args.append(am)
        member = jnp.where(hit, 1.0, member)
        work = jnp.where(hit, -jnp.inf, work)
    exps = [jnp.exp(tv - tops[0]) for tv in tops]
    esum = exps[0]
    for ev in exps[1:]:
        esum = esum + ev

    before = (lax.broadcasted_iota(jnp.int32, (tb, tb), 0) < lax.broadcasted_iota(jnp.int32, (tb, tb), 1))
    prefix = _dot(member.astype(BF16), jnp.where(before, 1.0, 0.0).astype(BF16))
    position = prefix + run_ref[:, 0:1]
    ti_ref[...] = jnp.zeros(ti_ref.shape, jnp.int32)
    gt_ref[...] = jnp.zeros(gt_ref.shape, F32)
    rk_ref[...] = jnp.zeros(rk_ref.shape, jnp.int32)
    for kk in range(TOP_K):
        ti_ref[kk:kk + 1, :] = args[kk].astype(jnp.int32)
        gt_ref[kk:kk + 1, :] = exps[kk] / esum
        rk = jnp.sum(jnp.where(eidx == args[kk], position, 0.0), axis=0, keepdims=True)
        rk_ref[kk:kk + 1, :] = rk.astype(jnp.int32)
    run_ref[...] = run_ref[...] + jnp.sum(member, axis=1, keepdims=True)
    cnt_ref[...] = run_ref[...]


def _xattn_call(l, x, k_all, v_all, wq, wo, l2g, l2b, rwt, rb, *, alpha, tb):
    bsz, t_len, d = x.shape
    m_len = k_all.shape[2]
    n_exp = rwt.shape[1]
    n_tok = bsz * t_len
    nt = t_len // tb
    assert t_len % tb == 0 and tb % LANES == 0
    vec = lambda n: pl.BlockSpec((None, 1, n), lambda i, t: (l, 0, 0))
    slot = pl.BlockSpec((SUBLANES, tb), lambda i, t: (0, i * nt + t))
    return pl.pallas_call(
        functools.partial(_xattn_kernel, alpha=alpha),
        out_shape=(
            jax.ShapeDtypeStruct((bsz, t_len, d), F32),
            jax.ShapeDtypeStruct((SUBLANES, n_tok), jnp.int32),
            jax.ShapeDtypeStruct((SUBLANES, n_tok), F32),
            jax.ShapeDtypeStruct((SUBLANES, n_tok), jnp.int32),
            jax.ShapeDtypeStruct((n_exp, LANES), F32),
        ),
        grid=(bsz, nt),
        in_specs=[
            pl.BlockSpec((None, tb, d), lambda i, t: (i, t, 0)),
            pl.BlockSpec((None, None, m_len, d), lambda i, t: (l, i, 0, 0)),
            pl.BlockSpec((None, None, m_len, d), lambda i, t: (l, i, 0, 0)),
            _resident((None, d, d), lambda i, t: (l, 0, 0)),
            _resident((None, d, d), lambda i, t: (l, 0, 0)),
            vec(d), vec(d),
            pl.BlockSpec((None, n_exp, d), lambda i, t: (l, 0, 0)),
            pl.BlockSpec((None, n_exp, 1), lambda i, t: (l, 0, 0)),
        ],
        out_specs=(
            pl.BlockSpec((None, tb, d), lambda i, t: (i, t, 0)),
            slot, slot, slot,
            pl.BlockSpec((n_exp, LANES), lambda i, t: (0, 0)),
        ),
        scratch_shapes=[pltpu.VMEM((n_exp, LANES), F32)],
        compiler_params=pltpu.CompilerParams(
            dimension_semantics=("arbitrary", "arbitrary"), vmem_limit_bytes=VMEM_LIMIT_BYTES),
        name="xattn_router",
    )(x, k_all, v_all, wq, wo, l2g, l2b, rwt, rb)


def _row_copy(src, src_row, dst, dst_row, sem):
    return pltpu.make_async_copy(src.at[pl.ds(src_row, 1)], dst.at[pl.ds(dst_row, 1)], sem)


def _dispatch_kernel(gs_ref, ti_ref, rk_ref, x_hbm, xs_hbm, sem):
    td = ti_ref.shape[1]
    i = pl.program_id(0)
    rows_per_step = TOP_K * td

    def drain():
        pltpu.make_async_copy(x_hbm.at[pl.ds(0, rows_per_step)], xs_hbm.at[pl.ds(0, rows_per_step)], sem).wait()

    @pl.when(i > 0)
    def _():
        drain()

    def body(j, carry):
        for kk in range(TOP_K):
            dst = gs_ref[ti_ref[kk, j]] + rk_ref[kk, j]
            _row_copy(x_hbm, i * td + j, xs_hbm, dst, sem).start()
        return carry

    lax.fori_loop(0, td, body, 0)

    @pl.when(i == pl.num_programs(0) - 1)
    def _():
        drain()


def _dispatch_call(gstart, top_idx, rank, x2d, *, td):
    n_tok, d = x2d.shape
    assert n_tok % td == 0
    smem_slot = pl.BlockSpec((SUBLANES, td), lambda i, gs: (0, i), memory_space=pltpu.SMEM)
    return pl.pallas_call(
        _dispatch_kernel,
        out_shape=jax.ShapeDtypeStruct((n_tok * TOP_K, d), F32),
        grid_spec=pltpu.PrefetchScalarGridSpec(
            num_scalar_prefetch=1,
            grid=(n_tok // td,),
            in_specs=[smem_slot, smem_slot, pl.BlockSpec(memory_space=pl.ANY)],
            out_specs=pl.BlockSpec(memory_space=pl.ANY),
            scratch_shapes=[pltpu.SemaphoreType.DMA],
        ),
        compiler_params=pltpu.CompilerParams(dimension_semantics=("arbitrary",)),
        name="dispatch",
    )(gstart, top_idx, rank, x2d)


def _expert_kernel(blk_ref, exp_ref, lo_ref, hi_ref, first_ref, xs_ref, wgu_ref, bgu_ref, wdn_ref, bdn_ref, ys_ref):
    i = pl.program_id(0)
    bm = xs_ref.shape[0]
    f = wdn_ref.shape[0]
    lo, hi = lo_ref[i], hi_ref[i]

    @pl.when(hi > lo)
    def _():
        gu = _dot(xs_ref[...].astype(BF16), wgu_ref[...]) + bgu_ref[...]
        gate = jnp.minimum(gu[:, :f], SWIGLU_LIMIT)
        up = jnp.clip(gu[:, f:], -SWIGLU_LIMIT, SWIGLU_LIMIT)
        act = (up + 1.0) * (gate * jax.nn.sigmoid(gate * SWIGLU_ALPHA))
        y = _dot(act.astype(BF16), wdn_ref[...]) + bdn_ref[...]
        row = lax.broadcasted_iota(jnp.int32, (bm, 1), 0)
        y = jnp.where(jnp.logical_and(row >= lo, row < hi), y, 0.0)

        @pl.when(first_ref[i] == 1)
        def _():
            ys_ref[...] = y

        @pl.when(first_ref[i] == 0)
        def _():
            ys_ref[...] = ys_ref[...] + y


def _expert_call(l, items, xs, wgu, bgu, wdn, bdn, *, bm):
    n_rows, d = xs.shape
    f = wdn.shape[2]
    n_items = items[0].shape[0]
    assert n_rows % bm == 0
    return pl.pallas_call(
        _expert_kernel,
        out_shape=jax.ShapeDtypeStruct((n_rows, d), F32),
        grid_spec=pltpu.PrefetchScalarGridSpec(
            num_scalar_prefetch=5,
            grid=(n_items,),
            in_specs=[
                pl.BlockSpec((bm, d), lambda i, blk, ex, lo, hi, fi: (blk[i], 0)),
                pl.BlockSpec((None, None, d, 2 * f), lambda i, blk, ex, lo, hi, fi: (l, ex[i], 0, 0)),
                pl.BlockSpec((None, None, 1, 2 * f), lambda i, blk, ex, lo, hi, fi: (l, ex[i], 0, 0)),
                pl.BlockSpec((None, None, f, d), lambda i, blk, ex, lo, hi, fi: (l, ex[i], 0, 0)),
                pl.BlockSpec((None, None, 1, d), lambda i, blk, ex, lo, hi, fi: (l, ex[i], 0, 0)),
            ],
            out_specs=pl.BlockSpec((bm, d), lambda i, blk, ex, lo, hi, fi: (blk[i], 0)),
        ),
        compiler_params=pltpu.CompilerParams(
            dimension_semantics=("arbitrary",), vmem_limit_bytes=VMEM_LIMIT_BYTES),
        name="experts",
    )(*items, xs, wgu, bgu, wdn, bdn)


def _expert_items(counts, n_rows, bm):
    n_exp = counts.shape[0]
    n_blocks = n_rows // bm
    n_items = n_blocks + n_exp - 1
    gend = jnp.cumsum(counts)
    gstart = gend - counts
    first_blk = gstart // bm
    nblk = jnp.where(counts > 0, (gend + bm - 1) // bm - first_blk, 0)
    item_end = jnp.cumsum(nblk)
    item_start = item_end - nblk
    it = jnp.arange(n_items, dtype=jnp.int32)
    e_raw = jnp.searchsorted(item_end, it, side="right").astype(jnp.int32)
    valid = e_raw < n_exp
    e = jnp.minimum(e_raw, n_exp - 1)
    blk = jnp.where(valid, first_blk[e] + it - item_start[e], n_blocks - 1)
    lo = jnp.where(valid, jnp.maximum(gstart[e], blk * bm) - blk * bm, 0)
    hi = jnp.where(valid, jnp.minimum(gend[e], (blk + 1) * bm) - blk * bm, 0)
    first = jnp.logical_and(valid, lo == 0)
    as_i32 = lambda a: a.astype(jnp.int32)
    return gstart.astype(jnp.int32), tuple(map(as_i32, (blk, e, lo, hi, first)))


def _combine_kernel(gs_ref, ti_ref, rk_ref, x_ref, gt_ref, l3g_ref, l3b_ref, ys_hbm, o_ref, rows, sem, *, alpha):
    tc, d = x_ref.shape

    def body(j, carry):
        for kk in range(TOP_K):
            src = gs_ref[ti_ref[kk, j]] + rk_ref[kk, j]
            _row_copy(ys_hbm, src, rows.at[kk], j, sem).start()
        return carry

    lax.fori_loop(0, tc, body, 0)
    for kk in range(TOP_K):
        pltpu.make_async_copy(ys_hbm.at[pl.ds(0, tc)], rows.at[kk], sem).wait()

    gates = jnp.transpose(gt_ref[...])
    acc = alpha * x_ref[...]
    for kk in range(TOP_K):
        acc = acc + gates[:, kk:kk + 1] * rows[kk]
    o_ref[...] = _layer_norm(acc, l3g_ref[...], l3b_ref[...])


def _combine_call(l, gstart, top_idx, rank, x2d, gates, l3g, l3b, ys, *, alpha, tc):
    n_tok, d = x2d.shape
    assert n_tok % tc == 0 and tc % LANES == 0
    smem_slot = pl.BlockSpec((SUBLANES, tc), lambda i, gs: (0, i), memory_space=pltpu.SMEM)
    vec = pl.BlockSpec((None, 1, d), lambda i, gs: (l, 0, 0))
    return pl.pallas_call(
        functools.partial(_combine_kernel, alpha=alpha),
        out_shape=jax.ShapeDtypeStruct((n_tok, d), F32),
        grid_spec=pltpu.PrefetchScalarGridSpec(
            num_scalar_prefetch=1,
            grid=(n_tok // tc,),
            in_specs=[
                smem_slot, smem_slot,
                pl.BlockSpec((tc, d), lambda i, gs: (i, 0)),
                pl.BlockSpec((SUBLANES, tc), lambda i, gs: (0, i)),
                vec, vec,
                pl.BlockSpec(memory_space=pl.ANY),
            ],
            out_specs=pl.BlockSpec((tc, d), lambda i, gs: (i, 0)),
            scratch_shapes=[pltpu.VMEM((TOP_K, tc, d), F32), pltpu.SemaphoreType.DMA],
        ),
        compiler_params=pltpu.CompilerParams(
            dimension_semantics=("arbitrary",), vmem_limit_bytes=VMEM_LIMIT_BYTES),
        name="combine",
    )(gstart, top_idx, rank, x2d, gates, l3g, l3b, ys)


def kernel(x, mem, mem_ln_g, mem_ln_b, w_in, b_in, conv_a_w, conv_a_b, ln_a_g, ln_a_b, w_a_out, b_a_out,
           conv_b_w, w_b_out, pool_w, pool_scale, w_mix_out, ln1_g, ln1_b, w_xq, w_xk, w_xv, w_xo, ln2_g, ln2_b,
           router_w, router_b, w_gu, b_gu, w_down, b_down, ln3_g, ln3_b):
    bsz, t_len, d = x.shape
    depth = w_in.shape[0]
    n_tok = bsz * t_len
    alpha = float((2 * depth) ** 0.25)
    tile = min(256, t_len)
    bm = min(512, n_tok * TOP_K)

    row = lambda a: a[:, None, :]
    bf = lambda a: a.astype(BF16)
    w_in_b, wao_b, wbo_b, pw_b, wmo_b = bf(w_in), bf(w_a_out), bf(w_b_out), bf(pool_w), bf(w_mix_out)
    wq_b, wo_b = bf(w_xq), bf(w_xo)
    rwt_b = bf(jnp.swapaxes(router_w, 1, 2))
    rb_c = router_b[:, :, None]
    wgu_b, wdn_b = bf(w_gu), bf(w_down)
    bgu_r, bdn_r = b_gu[:, :, None, :], b_down[:, :, None, :]

    k_all, v_all = _kv_call(mem, mem_ln_g[None, :], mem_ln_b[None, :], bf(w_xk), bf(w_xv))

    for l in range(depth):
        x1 = _mixer_call(l, x, w_in_b, row(b_in), conv_a_w, row(conv_a_b), row(ln_a_g), row(ln_a_b), wao_b,
                         row(b_a_out), conv_b_w, wbo_b, pw_b, row(pool_scale), wmo_b, row(ln1_g), row(ln1_b),
                         alpha=alpha, tb=tile)
        x2, top_idx, gates, rank, counts = _xattn_call(l, x1, k_all, v_all, wq_b, wo_b, row(ln2_g), row(ln2_b),
                                                       rwt_b, rb_c, alpha=alpha, tb=tile)
        gstart, items = _expert_items(counts[:, 0].astype(jnp.int32), n_tok * TOP_K, bm)
        x2d = x2.reshape(n_tok, d)
        xs = _dispatch_call(gstart, top_idx, rank, x2d, td=tile)
        ys = _expert_call(l, items, xs, wgu_b, bgu_r, wdn_b, bdn_r, bm=bm)
        x = _combine_call(l, gstart, top_idx, rank, x2d, gates, row(ln3_g), row(ln3_b), ys,
                          alpha=alpha, tc=tile).reshape(bsz, t_len, d)
    return x
```

```python
import functools

import jax
import jax.numpy as jnp
from jax import lax
from jax.experimental import pallas as pl
from jax.experimental.pallas import tpu as pltpu

N_XHEADS = 4
TOP_K = 4
POOL_WINDOWS = (2, 4, 8, 16)
SWIGLU_LIMIT = 7.0
SWIGLU_ALPHA = 1.702
LN_EPS = 1e-5

SUBLANES = 8
LANES = 128
VMEM_LIMIT_BYTES = 56 * 1024 * 1024

BF16 = jnp.bfloat16
F32 = jnp.float32


def _round_up(n, m):
    return (n + m - 1) // m * m


def _layer_norm(v, g, b):
    mu = jnp.mean(v, axis=-1, keepdims=True)
    c = v - mu
    var = jnp.mean(c * c, axis=-1, keepdims=True)
    return c * lax.rsqrt(var + LN_EPS) * g + b


def _dot(a, b):
    return jnp.dot(a, b, preferred_element_type=F32)


def _resident(block_shape, index_map):
    return pl.BlockSpec(block_shape, index_map, pipeline_mode=pl.Buffered(1))


def _kv_kernel(mem_ref, g_ref, b_ref, wk_ref, wv_ref, k_ref, v_ref):
    mn = _layer_norm(mem_ref[...], g_ref[...], b_ref[...]).astype(BF16)
    k_ref[...] = _dot(mn, wk_ref[...]).astype(BF16)
    v_ref[...] = _dot(mn, wv_ref[...]).astype(BF16)


def _kv_call(mem, g, b, wk, wv):
    bsz, m_len, d = mem.shape
    depth = wk.shape[0]
    out = jax.ShapeDtypeStruct((depth, bsz, m_len, d), BF16)
    return pl.pallas_call(
        _kv_kernel,
        out_shape=(out, out),
        grid=(depth, bsz),
        in_specs=[
            pl.BlockSpec((None, m_len, d), lambda l, i: (i, 0, 0)),
            pl.BlockSpec((1, d), lambda l, i: (0, 0)),
            pl.BlockSpec((1, d), lambda l, i: (0, 0)),
            pl.BlockSpec((None, d, d), lambda l, i: (l, 0, 0)),
            pl.BlockSpec((None, d, d), lambda l, i: (l, 0, 0)),
        ],
        out_specs=(
            pl.BlockSpec((None, None, m_len, d), lambda l, i: (l, i, 0, 0)),
            pl.BlockSpec((None, None, m_len, d), lambda l, i: (l, i, 0, 0)),
        ),
        compiler_params=pltpu.CompilerParams(
            dimension_semantics=("arbitrary", "arbitrary"), vmem_limit_bytes=VMEM_LIMIT_BYTES),
        name="mem_kv",
    )(mem, g, b, wk, wv)


def _causal_taps(buf, shifted, w_ref, halo, tb, row_chunk, out_ref):
    k_width = w_ref.shape[0]
    d = buf.shape[1]
    first = halo - (k_width - 1)
    n_rows = halo + tb - SUBLANES
    residues = sorted({o % SUBLANES for o in range(first, halo + 1)} - {0})
    for res in residues:
        shifted[res - 1, 0:n_rows, :] = buf[res:res + n_rows, :]
    for c0 in range(0, d, LANES):
        cols = slice(c0, c0 + LANES)
        for r0 in range(0, tb, row_chunk):
            acc = None
            for o in range(first, halo + 1):
                res = o % SUBLANES
                start = r0 + o - res
                rows = buf[start:start + row_chunk, cols] if res == 0 else shifted[res - 1, start:start + row_chunk, cols]
                term = w_ref[o - first:o - first + 1, cols] * rows
                acc = term if acc is None else acc + term
            out_ref[r0:r0 + row_chunk, cols] = acc


def _mixer_kernel(x_ref, w_in_ref, b_in_ref, caw_ref, cab_ref, lag_ref, lab_ref, wao_ref, bao_ref,
                  cbw_ref, wbo_ref, pw_ref, ps_ref, wmo_ref, l1g_ref, l1b_ref, o_ref,
                  abuf, bbuf, cbuf, tmp, shifted, *, alpha, halo_a, halo_b, halo_c):
    tb, d = x_ref.shape
    t = pl.program_id(1)

    @pl.when(t == 0)
    def _():
        abuf[0:halo_a, :] = jnp.zeros((halo_a, d), F32)
        bbuf[0:halo_b, :] = jnp.zeros((halo_b, d), F32)
        cbuf[0:halo_c, :] = jnp.zeros((halo_c, d), F32)

    x = x_ref[...]
    xb = x.astype(BF16)

    def proj(j):
        return _dot(xb, w_in_ref[:, j * d:(j + 1) * d]) + b_in_ref[:, j * d:(j + 1) * d]

    abuf[halo_a:halo_a + tb, :] = proj(0) * jax.nn.sigmoid(proj(1))
    _causal_taps(abuf, shifted, caw_ref, halo_a, tb, 128, tmp)
    a = _layer_norm(tmp[...] + cab_ref[...], lag_ref[...], lab_ref[...])
    a = a * jax.nn.sigmoid(a)
    y_a = _dot(a.astype(BF16), wao_ref[...]) + bao_ref[...]
    m = jax.nn.sigmoid(proj(6)) * y_a
    abuf[0:halo_a, :] = abuf[tb:tb + halo_a, :]

    bbuf[halo_b:halo_b + tb, :] = proj(3) * proj(4)
    kb = cbw_ref.shape[0]
    u_b = None
    for k in range(kb):
        term = cbw_ref[k:k + 1, :] * bbuf[halo_b - (kb - 1) + k:halo_b - (kb - 1) + k + tb, :]
        u_b = term if u_b is None else u_b + term
    y_b = _dot((proj(2) * u_b).astype(BF16), wbo_ref[...])
    m = m + jax.nn.sigmoid(proj(7)) * y_b
    bbuf[0:halo_b, :] = bbuf[tb:tb + halo_b, :]

    cbuf[halo_c:halo_c + tb, :] = proj(5)
    n_groups = len(POOL_WINDOWS)
    pg = d // n_groups
    pos = t * tb + lax.broadcasted_iota(jnp.int32, (tb, 1), 0)
    y_parts = []
    for gi, win in enumerate(POOL_WINDOWS):
        lo = gi * pg
        u = cbuf[halo_c:halo_c + tb, lo:lo + pg]
        s = u
        for j in range(1, win):
            s = s + cbuf[halo_c - j:halo_c - j + tb, lo:lo + pg]
        cnt = jnp.minimum(pos + 1, win).astype(F32)
        pooled = s / cnt - u
        y_parts.append(_dot(pooled.astype(BF16), pw_ref[gi]))
    y_c = jnp.concatenate(y_parts, axis=-1) * ps_ref[...]
    m = m + jax.nn.sigmoid(proj(8)) * y_c
    cbuf[0:halo_c, :] = cbuf[tb:tb + halo_c, :]

    mix = _dot(m.astype(BF16), wmo_ref[...])
    o_ref[...] = _layer_norm(alpha * x + mix, l1g_ref[...], l1b_ref[...])


def _mixer_call(l, x, w_in, b_in, caw, cab, lag, lab, wao, bao, cbw, wbo, pw, ps, wmo, l1g, l1b, *, alpha, tb):
    bsz, t_len, d = x.shape
    d_in = w_in.shape[2]
    ka, kb = caw.shape[1], cbw.shape[1]
    n_groups, pg = pw.shape[1], pw.shape[2]
    halo_a = _round_up(ka - 1, SUBLANES)
    halo_b = _round_up(kb - 1, SUBLANES)
    halo_c = _round_up(max(POOL_WINDOWS) - 1, SUBLANES)
    assert t_len % tb == 0 and tb % 128 == 0 and tb >= halo_a
    vec = lambda n: pl.BlockSpec((None, 1, n), lambda i, t: (l, 0, 0))
    kern = functools.partial(_mixer_kernel, alpha=alpha, halo_a=halo_a, halo_b=halo_b, halo_c=halo_c)
    return pl.pallas_call(
        kern,
        out_shape=jax.ShapeDtypeStruct((bsz, t_len, d), F32),
        grid=(bsz, t_len // tb),
        in_specs=[
            pl.BlockSpec((None, tb, d), lambda i, t: (i, t, 0)),
            _resident((None, d, d_in), lambda i, t: (l, 0, 0)),
            vec(d_in),
            pl.BlockSpec((None, ka, d), lambda i, t: (l, 0, 0)),
            vec(d), vec(d), vec(d),
            _resident((None, d, d), lambda i, t: (l, 0, 0)),
            vec(d),
            pl.BlockSpec((None, kb, d), lambda i, t: (l, 0, 0)),
            _resident((None, d, d), lambda i, t: (l, 0, 0)),
            _resident((None, n_groups, pg, pg), lambda i, t: (l, 0, 0, 0)),
            vec(d),
            _resident((None, d, d), lambda i, t: (l, 0, 0)),
            vec(d), vec(d),
        ],
        out_specs=pl.BlockSpec((None, tb, d), lambda i, t: (i, t, 0)),
        scratch_shapes=[
            pltpu.VMEM((halo_a + tb, d), F32),
            pltpu.VMEM((halo_b + tb, d), F32),
            pltpu.VMEM((halo_c + tb, d), F32),
            pltpu.VMEM((tb, d), F32),
            pltpu.VMEM((SUBLANES - 1, halo_a + tb - SUBLANES, d), F32),
        ],
        compiler_params=pltpu.CompilerParams(
            dimension_semantics=("arbitrary", "arbitrary"), vmem_limit_bytes=VMEM_LIMIT_BYTES),
        name="mixer",
    )(x, w_in, b_in, caw, cab, lag, lab, wao, bao, cbw, wbo, pw, ps, wmo, l1g, l1b)


def _xattn_kernel(x_ref, k_ref, v_ref, wq_ref, wo_ref, l2g_ref, l2b_ref, rwt_ref, rb_ref,
                  o_ref, ti_ref, gt_ref, rk_ref, cnt_ref, run_ref, *, alpha):
    tb, d = x_ref.shape
    n_exp = rwt_ref.shape[0]
    dh = d // N_XHEADS
    scale = dh ** -0.5
    first_step = jnp.logical_and(pl.program_id(0) == 0, pl.program_id(1) == 0)

    @pl.when(first_step)
    def _():
        run_ref[...] = jnp.zeros(run_ref.shape, F32)

    x = x_ref[...]
    q = _dot(x.astype(BF16), wq_ref[...])
    heads = []
    for h in range(N_XHEADS):
        qh = q[:, h * dh:(h + 1) * dh].astype(BF16)
        kh = k_ref[:, h * dh:(h + 1) * dh]
        vh = v_ref[:, h * dh:(h + 1) * dh]
        s = lax.dot_general(qh, kh, (((1,), (1,)), ((), ())), preferred_element_type=F32) * scale
        p = jnp.exp(s - jnp.max(s, axis=-1, keepdims=True))
        denom = jnp.sum(p, axis=-1, keepdims=True)
        heads.append(_dot(p.astype(BF16), vh) / denom)
    att = _dot(jnp.concatenate(heads, axis=-1).astype(BF16), wo_ref[...])
    x2 = _layer_norm(alpha * x + att, l2g_ref[...], l2b_ref[...])
    o_ref[...] = x2

    logits = lax.dot_general(rwt_ref[...], x2.astype(BF16), (((1,), (1,)), ((), ())),
                             preferred_element_type=F32) + rb_ref[...]
    eidx = lax.broadcasted_iota(jnp.int32, (n_exp, tb), 0).astype(F32)
    work = logits
    tops, args = [], []
    member = jnp.zeros((n_exp, tb), F32)
    for _ in range(TOP_K):
        mx = jnp.max(work, axis=0, keepdims=True)
        am = jnp.min(jnp.where(work == mx, eidx, float(n_exp)), axis=0, keepdims=True)
        hit = eidx == am
        tops.append(mx)
        args.append(am)
        member = jnp.where(hit, 1.0, member)
        work = jnp.where(hit, -jnp.inf, work)
    exps = [jnp.exp(tv - tops[0]) for tv in tops]
    esum = exps[0]
    for ev in exps[1:]:
        esum = esum + ev

    before = (lax.broadcasted_iota(jnp.int32, (tb, tb), 0) < lax.broadcasted_iota(jnp.int32, (tb, tb), 1))
    prefix = _dot(member.astype(BF16), jnp.where(before, 1.0, 0.0).astype(BF16))
    position = prefix + run_ref[:, 0:1]
    ti_ref[...] = jnp.zeros(ti_ref.shape, jnp.int32)
    gt_ref[...] = jnp.zeros(gt_ref.shape, F32)
    rk_ref[...] = jnp.zeros(rk_ref.shape, jnp.int32)
    for kk in range(TOP_K):
        ti_ref[kk:kk + 1, :] = args[kk].astype(jnp.int32)
        gt_ref[kk:kk + 1, :] = exps[kk] / esum
        rk = jnp.sum(jnp.where(eidx == args[kk], position, 0.0), axis=0, keepdims=True)
        rk_ref[kk:kk + 1, :] = rk.astype(jnp.int32)
    run_ref[...] = run_ref[...] + jnp.sum(member, axis=1, keepdims=True)
    cnt_ref[...] = run_ref[...]


def _xattn_call(l, x, k_all, v_all, wq, wo, l2g, l2b, rwt, rb, *, alpha, tb):
    bsz, t_len, d = x.shape
    m_len = k_all.shape[2]
    n_exp = rwt.shape[1]
    n_tok = bsz * t_len
    nt = t_len // tb
    assert t_len % tb == 0 and tb % LANES == 0
    vec = lambda n: pl.BlockSpec((None, 1, n), lambda i, t: (l, 0, 0))
    slot = pl.BlockSpec((SUBLANES, tb), lambda i, t: (0, i * nt + t))
    return pl.pallas_call(
        functools.partial(_xattn_kernel, alpha=alpha),
        out_shape=(
            jax.ShapeDtypeStruct((bsz, t_len, d), F32),
            jax.ShapeDtypeStruct((SUBLANES, n_tok), jnp.int32),
            jax.ShapeDtypeStruct((SUBLANES, n_tok), F32),
            jax.ShapeDtypeStruct((SUBLANES, n_tok), jnp.int32),
            jax.ShapeDtypeStruct((n_exp, LANES), F32),
        ),
        grid=(bsz, nt),
        in_specs=[
            pl.BlockSpec((None, tb, d), lambda i, t: (i, t, 0)),
            pl.BlockSpec((None, None, m_len, d), lambda i, t: (l, i, 0, 0)),
            pl.BlockSpec((None, None, m_len, d), lambda i, t: (l, i, 0, 0)),
            _resident((None, d, d), lambda i, t: (l, 0, 0)),
            _resident((None, d, d), lambda i, t: (l, 0, 0)),
            vec(d), vec(d),
            pl.BlockSpec((None, n_exp, d), lambda i, t: (l, 0, 0)),
            pl.BlockSpec((None, n_exp, 1), lambda i, t: (l, 0, 0)),
        ],
        out_specs=(
            pl.BlockSpec((None, tb, d), lambda i, t: (i, t, 0)),
            slot, slot, slot,
            pl.BlockSpec((n_exp, LANES), lambda i, t: (0, 0)),
        ),
        scratch_shapes=[pltpu.VMEM((n_exp, LANES), F32)],
        compiler_params=pltpu.CompilerParams(
            dimension_semantics=("arbitrary", "arbitrary"), vmem_limit_bytes=VMEM_LIMIT_BYTES),
        name="xattn_router",
    )(x, k_all, v_all, wq, wo, l2g, l2b, rwt, rb)


def _row_copy(src, src_row, dst, dst_row, sem):
    return pltpu.make_async_copy(src.at[pl.ds(src_row, 1)], dst.at[pl.ds(dst_row, 1)], sem)


def _dispatch_kernel(gs_ref, ti_ref, rk_ref, x_ref, xs_hbm, sem):
    td = x_ref.shape[0]

    def body(j, carry):
        for kk in range(TOP_K):
            dst = gs_ref[ti_ref[kk, j]] + rk_ref[kk, j]
            _row_copy(x_ref, j, xs_hbm, dst, sem).start()
        return carry

    lax.fori_loop(0, td, body, 0)
    for kk in range(TOP_K):
        pltpu.make_async_copy(x_ref, xs_hbm.at[pl.ds(0, td)], sem).wait()


def _dispatch_call(gstart, top_idx, rank, x2d, *, td):
    n_tok, d = x2d.shape
    assert n_tok % td == 0
    smem_slot = pl.BlockSpec((SUBLANES, td), lambda i, gs: (0, i), memory_space=pltpu.SMEM)
    return pl.pallas_call(
        _dispatch_kernel,
        out_shape=jax.ShapeDtypeStruct((n_tok * TOP_K, d), F32),
        grid_spec=pltpu.PrefetchScalarGridSpec(
            num_scalar_prefetch=1,
            grid=(n_tok // td,),
            in_specs=[smem_slot, smem_slot, pl.BlockSpec((td, d), lambda i, gs: (i, 0))],
            out_specs=pl.BlockSpec(memory_space=pl.ANY),
            scratch_shapes=[pltpu.SemaphoreType.DMA],
        ),
        compiler_params=pltpu.CompilerParams(dimension_semantics=("arbitrary",)),
        name="dispatch",
    )(gstart, top_idx, rank, x2d)


def _expert_kernel(blk_ref, exp_ref, lo_ref, hi_ref, first_ref, xs_ref, wgu_ref, bgu_ref, wdn_ref, bdn_ref, ys_ref):
    i = pl.program_id(0)
    bm = xs_ref.shape[0]
    f = wdn_ref.shape[0]
    lo, hi = lo_ref[i], hi_ref[i]

    @pl.when(hi > lo)
    def _():
        gu = _dot(xs_ref[...].astype(BF16), wgu_ref[...]) + bgu_ref[...]
        gate = jnp.minimum(gu[:, :f], SWIGLU_LIMIT)
        up = jnp.clip(gu[:, f:], -SWIGLU_LIMIT, SWIGLU_LIMIT)
        act = (up + 1.0) * (gate * jax.nn.sigmoid(gate * SWIGLU_ALPHA))
        y = _dot(act.astype(BF16), wdn_ref[...]) + bdn_ref[...]
        row = lax.broadcasted_iota(jnp.int32, (bm, 1), 0)
        y = jnp.where(jnp.logical_and(row >= lo, row < hi), y, 0.0)

        @pl.when(first_ref[i] == 1)
        def _():
            ys_ref[...] = y

        @pl.when(first_ref[i] == 0)
        def _():
            ys_ref[...] = ys_ref[...] + y


def _expert_call(l, items, xs, wgu, bgu, wdn, bdn, *, bm):
    n_rows, d = xs.shape
    f = wdn.shape[2]
    n_items = items[0].shape[0]
    assert n_rows % bm == 0
    return pl.pallas_call(
        _expert_kernel,
        out_shape=jax.ShapeDtypeStruct((n_rows, d), F32),
        grid_spec=pltpu.PrefetchScalarGridSpec(
            num_scalar_prefetch=5,
            grid=(n_items,),
            in_specs=[
                pl.BlockSpec((bm, d), lambda i, blk, ex, lo, hi, fi: (blk[i], 0)),
                pl.BlockSpec((None, None, d, 2 * f), lambda i, blk, ex, lo, hi, fi: (l, ex[i], 0, 0)),
                pl.BlockSpec((None, None, 1, 2 * f), lambda i, blk, ex, lo, hi, fi: (l, ex[i], 0, 0)),
                pl.BlockSpec((None, None, f, d), lambda i, blk, ex, lo, hi, fi: (l, ex[i], 0, 0)),
                pl.BlockSpec((None, None, 1, d), lambda i, blk, ex, lo, hi, fi: (l, ex[i], 0, 0)),
            ],
            out_specs=pl.BlockSpec((bm, d), lambda i, blk, ex, lo, hi, fi: (blk[i], 0)),
        ),
        compiler_params=pltpu.CompilerParams(
            dimension_semantics=("arbitrary",), vmem_limit_bytes=VMEM_LIMIT_BYTES),
        name="experts",
    )(*items, xs, wgu, bgu, wdn, bdn)


def _expert_items(counts, n_rows, bm):
    n_exp = counts.shape[0]
    n_blocks = n_rows // bm
    n_items = n_blocks + n_exp - 1
    gend = jnp.cumsum(counts)
    gstart = gend - counts
    first_blk = gstart // bm
    nblk = jnp.where(counts > 0, (gend + bm - 1) // bm - first_blk, 0)
    item_end = jnp.cumsum(nblk)
    item_start = item_end - nblk
    it = jnp.arange(n_items, dtype=jnp.int32)
    e_raw = jnp.sum(item_end[None, :] <= it[:, None], axis=1).astype(jnp.int32)
    valid = e_raw < n_exp
    e = jnp.minimum(e_raw, n_exp - 1)
    blk = jnp.where(valid, first_blk[e] + it - item_start[e], n_blocks - 1)
    lo = jnp.where(valid, jnp.maximum(gstart[e], blk * bm) - blk * bm, 0)
    hi = jnp.where(valid, jnp.minimum(gend[e], (blk + 1) * bm) - blk * bm, 0)
    first = jnp.logical_and(valid, lo == 0)
    as_i32 = lambda a: a.astype(jnp.int32)
    return gstart.astype(jnp.int32), tuple(map(as_i32, (blk, e, lo, hi, first)))


def _combine_kernel(gs_ref, ti_ref, rk_ref, x_ref, gt_ref, l3g_ref, l3b_ref, ys_hbm, o_ref, rows, sem, *, alpha):
    tc, d = x_ref.shape

    def body(j, carry):
        for kk in range(TOP_K):
            src = gs_ref[ti_ref[kk, j]] + rk_ref[kk, j]
            _row_copy(ys_hbm, src, rows.at[kk], j, sem).start()
        return carry

    lax.fori_loop(0, tc, body, 0)
    for kk in range(TOP_K):
        pltpu.make_async_copy(ys_hbm.at[pl.ds(0, tc)], rows.at[kk], sem).wait()

    gates = jnp.transpose(gt_ref[...])
    acc = alpha * x_ref[...]
    for kk in range(TOP_K):
        acc = acc + gates[:, kk:kk + 1] * rows[kk]
    o_ref[...] = _layer_norm(acc, l3g_ref[...], l3b_ref[...])


def _combine_call(l, gstart, top_idx, rank, x2d, gates, l3g, l3b, ys, *, alpha, tc):
    n_tok, d = x2d.shape
    assert n_tok % tc == 0 and tc % LANES == 0
    smem_slot = pl.BlockSpec((SUBLANES, tc), lambda i, gs: (0, i), memory_space=pltpu.SMEM)
    vec = pl.BlockSpec((None, 1, d), lambda i, gs: (l, 0, 0))
    return pl.pallas_call(
        functools.partial(_combine_kernel, alpha=alpha),
        out_shape=jax.ShapeDtypeStruct((n_tok, d), F32),
        grid_spec=pltpu.PrefetchScalarGridSpec(
            num_scalar_prefetch=1,
            grid=(n_tok // tc,),
            in_specs=[
                smem_slot, smem_slot,
                pl.BlockSpec((tc, d), lambda i, gs: (i, 0)),
                pl.BlockSpec((SUBLANES, tc), lambda i, gs: (0, i)),
                vec, vec,
                pl.BlockSpec(memory_space=pl.ANY),
            ],
            out_specs=pl.BlockSpec((tc, d), lambda i, gs: (i, 0)),
            scratch_shapes=[pltpu.VMEM((TOP_K, tc, d), F32), pltpu.SemaphoreType.DMA],
        ),
        compiler_params=pltpu.CompilerParams(
            dimension_semantics=("arbitrary",), vmem_limit_bytes=VMEM_LIMIT_BYTES),
        name="combine",
    )(gstart, top_idx, rank, x2d, gates, l3g, l3b, ys)


def kernel(x, mem, mem_ln_g, mem_ln_b, w_in, b_in, conv_a_w, conv_a_b, ln_a_g, ln_a_b, w_a_out, b_a_out,
           conv_b_w, w_b_out, pool_w, pool_scale, w_mix_out, ln1_g, ln1_b, w_xq, w_xk, w_xv, w_xo, ln2_g, ln2_b,
           router_w, router_b, w_gu, b_gu, w_down, b_down, ln3_g, ln3_b):
    bsz, t_len, d = x.shape
    depth = w_in.shape[0]
    n_tok = bsz * t_len
    alpha = float((2 * depth) ** 0.25)
    tile = min(256, t_len)
    bm = min(512, n_tok * TOP_K)

    row = lambda a: a[:, None, :]
    bf = lambda a: a.astype(BF16)
    w_in_b, wao_b, wbo_b, pw_b, wmo_b = bf(w_in), bf(w_a_out), bf(w_b_out), bf(pool_w), bf(w_mix_out)
    wq_b, wo_b = bf(w_xq), bf(w_xo)
    rwt_b = bf(jnp.swapaxes(router_w, 1, 2))
    rb_c = router_b[:, :, None]
    wgu_b, wdn_b = bf(w_gu), bf(w_down)
    bgu_r, bdn_r = b_gu[:, :, None, :], b_down[:, :, None, :]

    k_all, v_all = _kv_call(mem, mem_ln_g[None, :], mem_ln_b[None, :], bf(w_xk), bf(w_xv))

    for l in range(depth):
        x1 = _mixer_call(l, x, w_in_b, row(b_in), conv_a_w, row(conv_a_b), row(ln_a_g), row(ln_a_b), wao_b,
                         row(b_a_out), conv_b_w, wbo_b, pw_b, row(pool_scale), wmo_b, row(ln1_g), row(ln1_b),
                         alpha=alpha, tb=tile)
        x2, top_idx, gates, rank, counts = _xattn_call(l, x1, k_all, v_all, wq_b, wo_b, row(ln2_g), row(ln2_b),
                                                       rwt_b, rb_c, alpha=alpha, tb=tile)
        gstart, items = _expert_items(counts[:, 0].astype(jnp.int32), n_tok * TOP_K, bm)
        x2d = x2.reshape(n_tok, d)
        xs = _dispatch_call(gstart, top_idx, rank, x2d, td=min(512, n_tok))
        ys = _expert_call(l, items, xs, wgu_b, bgu_r, wdn_b, bdn_r, bm=bm)
        x = _combine_call(l, gstart, top_idx, rank, x2d, gates, row(ln3_g), row(ln3_b), ys,
                          alpha=alpha, tc=tile).reshape(bsz, t_len, d)
    return x
```

```python
import functools

import jax
import jax.numpy as jnp
from jax import lax
from jax.experimental import pallas as pl
from jax.experimental.pallas import tpu as pltpu

N_XHEADS = 4
TOP_K = 4
POOL_WINDOWS = (2, 4, 8, 16)
SWIGLU_LIMIT = 7.0
SWIGLU_ALPHA = 1.702
LN_EPS = 1e-5

SUBLANES = 8
LANES = 128
VMEM_LIMIT_BYTES = 56 * 1024 * 1024

BF16 = jnp.bfloat16
F32 = jnp.float32


def _round_up(n, m):
    return (n + m - 1) // m * m


def _layer_norm(v, g, b):
    mu = jnp.mean(v, axis=-1, keepdims=True)
    c = v - mu
    var = jnp.mean(c * c, axis=-1, keepdims=True)
    return c * lax.rsqrt(var + LN_EPS) * g + b


def _dot(a, b):
    return jnp.dot(a, b, preferred_element_type=F32)


def _resident(block_shape, index_map):
    return pl.BlockSpec(block_shape, index_map, pipeline_mode=pl.Buffered(1))


def _kv_kernel(mem_ref, g_ref, b_ref, wk_ref, wv_ref, k_ref, v_ref):
    mn = _layer_norm(mem_ref[...], g_ref[...], b_ref[...]).astype(BF16)
    k_ref[...] = _dot(mn, wk_ref[...]).astype(BF16)
    v_ref[...] = _dot(mn, wv_ref[...]).astype(BF16)


def _kv_call(mem, g, b, wk, wv):
    bsz, m_len, d = mem.shape
    depth = wk.shape[0]
    out = jax.ShapeDtypeStruct((depth, bsz, m_len, d), BF16)
    return pl.pallas_call(
        _kv_kernel,
        out_shape=(out, out),
        grid=(depth, bsz),
        in_specs=[
            pl.BlockSpec((None, m_len, d), lambda l, i: (i, 0, 0)),
            pl.BlockSpec((1, d), lambda l, i: (0, 0)),
            pl.BlockSpec((1, d), lambda l, i: (0, 0)),
            pl.BlockSpec((None, d, d), lambda l, i: (l, 0, 0)),
            pl.BlockSpec((None, d, d), lambda l, i: (l, 0, 0)),
        ],
        out_specs=(
            pl.BlockSpec((None, None, m_len, d), lambda l, i: (l, i, 0, 0)),
            pl.BlockSpec((None, None, m_len, d), lambda l, i: (l, i, 0, 0)),
        ),
        compiler_params=pltpu.CompilerParams(
            dimension_semantics=("arbitrary", "arbitrary"), vmem_limit_bytes=VMEM_LIMIT_BYTES),
        name="mem_kv",
    )(mem, g, b, wk, wv)


def _causal_taps(buf, shifted, w_ref, halo, tb, row_chunk, out_ref):
    k_width = w_ref.shape[0]
    d = buf.shape[1]
    first = halo - (k_width - 1)
    n_rows = halo + tb - SUBLANES
    residues = sorted({o % SUBLANES for o in range(first, halo + 1)} - {0})
    for res in residues:
        shifted[res - 1, 0:n_rows, :] = buf[res:res + n_rows, :]
    for c0 in range(0, d, LANES):
        cols = slice(c0, c0 + LANES)
        for r0 in range(0, tb, row_chunk):
            acc = None
            for o in range(first, halo + 1):
                res = o % SUBLANES
                start = r0 + o - res
                rows = buf[start:start + row_chunk, cols] if res == 0 else shifted[res - 1, start:start + row_chunk, cols]
                term = w_ref[o - first:o - first + 1, cols] * rows
                acc = term if acc is None else acc + term
            out_ref[r0:r0 + row_chunk, cols] = acc


def _mixer_kernel(x_ref, w_in_ref, b_in_ref, caw_ref, cab_ref, lag_ref, lab_ref, wao_ref, bao_ref,
                  cbw_ref, wbo_ref, pw_ref, ps_ref, wmo_ref, l1g_ref, l1b_ref, o_ref,
                  abuf, bbuf, cbuf, tmp, shifted, *, alpha, halo_a, halo_b, halo_c):
    tb, d = x_ref.shape
    t = pl.program_id(1)

    @pl.when(t == 0)
    def _():
        abuf[0:halo_a, :] = jnp.zeros((halo_a, d), F32)
        bbuf[0:halo_b, :] = jnp.zeros((halo_b, d), F32)
        cbuf[0:halo_c, :] = jnp.zeros((halo_c, d), F32)

    x = x_ref[...]
    xb = x.astype(BF16)

    def proj(j):
        return _dot(xb, w_in_ref[:, j * d:(j + 1) * d]) + b_in_ref[:, j * d:(j + 1) * d]

    abuf[halo_a:halo_a + tb, :] = proj(0) * jax.nn.sigmoid(proj(1))
    _causal_taps(abuf, shifted, caw_ref, halo_a, tb, 128, tmp)
    a = _layer_norm(tmp[...] + cab_ref[...], lag_ref[...], lab_ref[...])
    a = a * jax.nn.sigmoid(a)
    y_a = _dot(a.astype(BF16), wao_ref[...]) + bao_ref[...]
    m = jax.nn.sigmoid(proj(6)) * y_a
    abuf[0:halo_a, :] = abuf[tb:tb + halo_a, :]

    bbuf[halo_b:halo_b + tb, :] = proj(3) * proj(4)
    kb = cbw_ref.shape[0]
    u_b = None
    for k in range(kb):
        term = cbw_ref[k:k + 1, :] * bbuf[halo_b - (kb - 1) + k:halo_b - (kb - 1) + k + tb, :]
        u_b = term if u_b is None else u_b + term
    y_b = _dot((proj(2) * u_b).astype(BF16), wbo_ref[...])
    m = m + jax.nn.sigmoid(proj(7)) * y_b
    bbuf[0:halo_b, :] = bbuf[tb:tb + halo_b, :]

    cbuf[halo_c:halo_c + tb, :] = proj(5)
    n_groups = len(POOL_WINDOWS)
    pg = d // n_groups
    pos = t * tb + lax.broadcasted_iota(jnp.int32, (tb, 1), 0)
    y_parts = []
    for gi, win in enumerate(POOL_WINDOWS):
        lo = gi * pg
        u = cbuf[halo_c:halo_c + tb, lo:lo + pg]
        s = u
        for j in range(1, win):
            s = s + cbuf[halo_c - j:halo_c - j + tb, lo:lo + pg]
        cnt = jnp.minimum(pos + 1, win).astype(F32)
        pooled = s / cnt - u
        y_parts.append(_dot(pooled.astype(BF16), pw_ref[gi]))
    y_c = jnp.concatenate(y_parts, axis=-1) * ps_ref[...]
    m = m + jax.nn.sigmoid(proj(8)) * y_c
    cbuf[0:halo_c, :] = cbuf[tb:tb + halo_c, :]

    mix = _dot(m.astype(BF16), wmo_ref[...])
    o_ref[...] = _layer_norm(alpha * x + mix, l1g_ref[...], l1b_ref[...])


def _mixer_call(l, x, w_in, b_in, caw, cab, lag, lab, wao, bao, cbw, wbo, pw, ps, wmo, l1g, l1b, *, alpha, tb):
    bsz, t_len, d = x.shape
    d_in = w_in.shape[2]
    ka, kb = caw.shape[1], cbw.shape[1]
    n_groups, pg = pw.shape[1], pw.shape[2]
    halo_a = _round_up(ka - 1, SUBLANES)
    halo_b = _round_up(kb - 1, SUBLANES)
    halo_c = _round_up(max(POOL_WINDOWS) - 1, SUBLANES)
    assert t_len % tb == 0 and tb % 128 == 0 and tb >= halo_a
    vec = lambda n: pl.BlockSpec((None, 1, n), lambda i, t: (l, 0, 0))
    kern = functools.partial(_mixer_kernel, alpha=alpha, halo_a=halo_a, halo_b=halo_b, halo_c=halo_c)
    return pl.pallas_call(
        kern,
        out_shape=jax.ShapeDtypeStruct((bsz, t_len, d), F32),
        grid=(bsz, t_len // tb),
        in_specs=[
            pl.BlockSpec((None, tb, d), lambda i, t: (i, t, 0)),
            _resident((None, d, d_in), lambda i, t: (l, 0, 0)),
            vec(d_in),
            pl.BlockSpec((None, ka, d), lambda i, t: (l, 0, 0)),
            vec(d), vec(d), vec(d),
            _resident((None, d, d), lambda i, t: (l, 0, 0)),
            vec(d),
            pl.BlockSpec((None, kb, d), lambda i, t: (l, 0, 0)),
            _resident((None, d, d), lambda i, t: (l, 0, 0)),
            _resident((None, n_groups, pg, pg), lambda i, t: (l, 0, 0, 0)),
            vec(d),
            _resident((None, d, d), lambda i, t: (l, 0, 0)),
            vec(d), vec(d),
        ],
        out_specs=pl.BlockSpec((None, tb, d), lambda i, t: (i, t, 0)),
        scratch_shapes=[
            pltpu.VMEM((halo_a + tb, d), F32),
            pltpu.VMEM((halo_b + tb, d), F32),
            pltpu.VMEM((halo_c + tb, d), F32),
            pltpu.VMEM((tb, d), F32),
            pltpu.VMEM((SUBLANES - 1, halo_a + tb - SUBLANES, d), F32),
        ],
        compiler_params=pltpu.CompilerParams(
            dimension_semantics=("arbitrary", "arbitrary"), vmem_limit_bytes=VMEM_LIMIT_BYTES),
        name="mixer",
    )(x, w_in, b_in, caw, cab, lag, lab, wao, bao, cbw, wbo, pw, ps, wmo, l1g, l1b)


def _xattn_kernel(x_ref, k_ref, v_ref, wq_ref, wo_ref, l2g_ref, l2b_ref, rwt_ref, rb_ref,
                  o_ref, gt_ref, ld_ref, cp_ref, *, alpha):
    tb, d = x_ref.shape
    n_exp = rwt_ref.shape[0]
    dh = d // N_XHEADS
    scale = dh ** -0.5

    x = x_ref[...]
    q = _dot(x.astype(BF16), wq_ref[...])
    heads = []
    for h in range(N_XHEADS):
        qh = q[:, h * dh:(h + 1) * dh].astype(BF16)
        kh = k_ref[:, h * dh:(h + 1) * dh]
        vh = v_ref[:, h * dh:(h + 1) * dh]
        s = lax.dot_general(qh, kh, (((1,), (1,)), ((), ())), preferred_element_type=F32) * scale
        p = jnp.exp(s - jnp.max(s, axis=-1, keepdims=True))
        denom = jnp.sum(p, axis=-1, keepdims=True)
        heads.append(_dot(p.astype(BF16), vh) / denom)
    att = _dot(jnp.concatenate(heads, axis=-1).astype(BF16), wo_ref[...])
    x2 = _layer_norm(alpha * x + att, l2g_ref[...], l2b_ref[...])
    o_ref[...] = x2

    logits = lax.dot_general(rwt_ref[...], x2.astype(BF16), (((1,), (1,)), ((), ())),
                             preferred_element_type=F32) + rb_ref[...]
    eidx = lax.broadcasted_iota(jnp.int32, (n_exp, tb), 0).astype(F32)
    work = logits
    tops, args = [], []
    member = jnp.zeros((n_exp, tb), F32)
    for _ in range(TOP_K):
        mx = jnp.max(work, axis=0, keepdims=True)
        am = jnp.min(jnp.where(work == mx, eidx, float(n_exp)), axis=0, keepdims=True)
        hit = eidx == am
        tops.append(mx)
        args.append(am)
        member = jnp.where(hit, 1.0, member)
        work = jnp.where(hit, -jnp.inf, work)
    exps = [jnp.exp(tv - tops[0]) for tv in tops]
    esum = exps[0]
    for ev in exps[1:]:
        esum = esum + ev

    before = lax.broadcasted_iota(jnp.int32, (tb, tb), 0) < lax.broadcasted_iota(jnp.int32, (tb, tb), 1)
    prefix = _dot(member.astype(BF16), jnp.where(before, 1.0, 0.0).astype(BF16))
    count = jnp.sum(member, axis=1, keepdims=True)
    padded = jnp.floor((count + (SUBLANES - 1)) * (1.0 / SUBLANES)) * SUBLANES
    padded = jnp.broadcast_to(padded, (n_exp, LANES))
    lower = lax.broadcasted_iota(jnp.int32, (n_exp, n_exp), 1) < lax.broadcasted_iota(jnp.int32, (n_exp, n_exp), 0)
    run_start = _dot(jnp.where(lower, 1.0, 0.0).astype(BF16), padded.astype(BF16))
    position = prefix + run_start[:, 0:1]
    gt_ref[...] = jnp.zeros(gt_ref.shape, F32)
    ld_ref[...] = jnp.zeros(ld_ref.shape, jnp.int32)
    for kk in range(TOP_K):
        gt_ref[kk:kk + 1, :] = exps[kk] / esum
        pos = jnp.sum(jnp.where(eidx == args[kk], position, 0.0), axis=0, keepdims=True)
        ld_ref[kk:kk + 1, :] = pos.astype(jnp.int32)
    cp_ref[...] = padded


def _xattn_call(l, x, k_all, v_all, wq, wo, l2g, l2b, rwt, rb, *, alpha, tb):
    bsz, t_len, d = x.shape
    m_len = k_all.shape[2]
    n_exp = rwt.shape[1]
    n_tok = bsz * t_len
    nt = t_len // tb
    assert t_len % tb == 0 and tb % LANES == 0
    vec = lambda n: pl.BlockSpec((None, 1, n), lambda i, t: (l, 0, 0))
    slot = pl.BlockSpec((SUBLANES, tb), lambda i, t: (0, i * nt + t))
    return pl.pallas_call(
        functools.partial(_xattn_kernel, alpha=alpha),
        out_shape=(
            jax.ShapeDtypeStruct((bsz, t_len, d), F32),
            jax.ShapeDtypeStruct((SUBLANES, n_tok), F32),
            jax.ShapeDtypeStruct((SUBLANES, n_tok), jnp.int32),
            jax.ShapeDtypeStruct((n_tok // tb, n_exp, LANES), F32),
        ),
        grid=(bsz, nt),
        in_specs=[
            pl.BlockSpec((None, tb, d), lambda i, t: (i, t, 0)),
            pl.BlockSpec((None, None, m_len, d), lambda i, t: (l, i, 0, 0)),
            pl.BlockSpec((None, None, m_len, d), lambda i, t: (l, i, 0, 0)),
            _resident((None, d, d), lambda i, t: (l, 0, 0)),
            _resident((None, d, d), lambda i, t: (l, 0, 0)),
            vec(d), vec(d),
            pl.BlockSpec((None, n_exp, d), lambda i, t: (l, 0, 0)),
            pl.BlockSpec((None, n_exp, 1), lambda i, t: (l, 0, 0)),
        ],
        out_specs=(
            pl.BlockSpec((None, tb, d), lambda i, t: (i, t, 0)),
            slot, slot,
            pl.BlockSpec((None, n_exp, LANES), lambda i, t: (i * nt + t, 0, 0)),
        ),
        compiler_params=pltpu.CompilerParams(
            dimension_semantics=("arbitrary", "arbitrary"), vmem_limit_bytes=VMEM_LIMIT_BYTES),
        name="xattn_router",
    )(x, k_all, v_all, wq, wo, l2g, l2b, rwt, rb)


def _routing_plan(cp, n_blocks, bm):
    n_exp = cp.shape[1]
    local_start = jnp.cumsum(cp, axis=1) - cp
    tot = jnp.sum(cp, axis=0)
    gend = jnp.cumsum(tot)
    gstart = gend - tot
    global_start = gstart[None, :] + jnp.cumsum(cp, axis=0) - cp
    total = gend[-1]
    tile_chunks = jnp.sum(cp, axis=1) // SUBLANES

    n_items = n_blocks + n_exp + 1
    first_blk = gstart // bm
    nblk = jnp.where(tot > 0, (gend + bm - 1) // bm - first_blk, 0)
    item_end = jnp.cumsum(nblk)
    item_start = item_end - nblk
    it = jnp.arange(n_items, dtype=jnp.int32)
    e_raw = jnp.sum(item_end[None, :] <= it[:, None], axis=1).astype(jnp.int32)
    valid = e_raw < n_exp
    e = jnp.minimum(e_raw, n_exp - 1)
    unused_blk = jnp.minimum((total - 1) // bm + 1 + it - item_end[-1], n_blocks)
    blk = jnp.where(valid, first_blk[e] + it - item_start[e], unused_blk)
    lo = jnp.where(valid, jnp.maximum(gstart[e], blk * bm) - blk * bm, 0)
    hi = jnp.where(valid, jnp.minimum(gend[e], (blk + 1) * bm) - blk * bm, 0)
    first = jnp.logical_or(jnp.logical_not(valid), lo == 0)
    i32 = lambda a: a.astype(jnp.int32)
    runs = tuple(i32(a.reshape(-1)) for a in (cp // SUBLANES, local_start, global_start)) + (i32(tile_chunks),)
    return runs, i32(total.reshape(1)), tuple(map(i32, (blk, e, lo, hi, first)))


def _run_copies(nc_ref, ls_ref, gs_ref, step, n_exp, make_copy, act):
    def per_expert(e, carry):
        idx = step * n_exp + e
        local0, global0 = ls_ref[idx], gs_ref[idx]

        def per_chunk(c, carry2):
            off = c * SUBLANES
            act(make_copy(pl.multiple_of(local0 + off, SUBLANES), pl.multiple_of(global0 + off, SUBLANES)))
            return carry2

        lax.fori_loop(0, nc_ref[idx], per_chunk, 0)
        return carry

    lax.fori_loop(0, n_exp, per_expert, 0)


def _wait_chunks(n_chunks, make_copy):
    def body(c, carry):
        make_copy(0, 0).wait()
        return carry

    lax.fori_loop(0, n_chunks, body, 0)


def _dispatch_kernel(nc_ref, ls_ref, gs_ref, chunks_ref, total_ref, ld_ref, x_ref, xs_hbm, stage, zeros, sems, zsem,
                     *, n_exp):
    s = pl.program_id(0)
    last = pl.num_programs(0) - 1
    tb, d = x_ref.shape
    n_rows = stage.shape[1]
    slot = s % 2

    def chunk_copy(slot_):
        def make(local_row, global_row):
            return pltpu.make_async_copy(stage.at[slot_, pl.ds(local_row, SUBLANES)],
                                         xs_hbm.at[pl.ds(global_row, SUBLANES)], sems.at[slot_])
        return make

    rid = lax.broadcasted_iota(jnp.int32, (n_rows, tb), 0)
    onehot = jnp.zeros((n_rows, tb), F32)
    for kk in range(TOP_K):
        onehot = jnp.where(rid == ld_ref[kk:kk + 1, :], 1.0, onehot)
    stage[slot] = _dot(onehot.astype(BF16), x_ref[...].astype(BF16))

    _run_copies(nc_ref, ls_ref, gs_ref, s, n_exp, chunk_copy(slot), lambda c: c.start())

    @pl.when(s > 0)
    def _():
        _wait_chunks(chunks_ref[s - 1], chunk_copy(1 - slot))

    @pl.when(s == last)
    def _():
        _wait_chunks(chunks_ref[s], chunk_copy(slot))
        zeros[...] = jnp.zeros(zeros.shape, F32)
        zrows = zeros.shape[0]
        total = total_ref[0]
        rest = xs_hbm.shape[0] - total
        n_big = rest // zrows
        tail0 = total + n_big * zrows

        def big(j):
            return pltpu.make_async_copy(
                zeros, xs_hbm.at[pl.ds(pl.multiple_of(total + j * zrows, SUBLANES), zrows)], zsem)

        def small(j):
            return pltpu.make_async_copy(
                zeros.at[pl.ds(0, SUBLANES)],
                xs_hbm.at[pl.ds(pl.multiple_of(tail0 + j * SUBLANES, SUBLANES), SUBLANES)], zsem)

        for make, count in ((big, n_big), (small, (rest - n_big * zrows) // SUBLANES)):
            lax.fori_loop(0, count, lambda j, c, make=make: (make(j).start(), c)[1], 0)
            lax.fori_loop(0, count, lambda j, c, make=make: (make(j).wait(), c)[1], 0)


def _dispatch_call(runs, total, ldest, x2d, *, n_exp, n_rows_out, stage_rows, tb, bm):
    n_tok, d = x2d.shape
    assert n_tok % tb == 0
    return pl.pallas_call(
        functools.partial(_dispatch_kernel, n_exp=n_exp),
        out_shape=jax.ShapeDtypeStruct((n_rows_out, d), F32),
        grid_spec=pltpu.PrefetchScalarGridSpec(
            num_scalar_prefetch=5,
            grid=(n_tok // tb,),
            in_specs=[pl.BlockSpec((SUBLANES, tb), lambda i, *_: (0, i)),
                      pl.BlockSpec((tb, d), lambda i, *_: (i, 0))],
            out_specs=pl.BlockSpec(memory_space=pl.ANY),
            scratch_shapes=[pltpu.VMEM((2, stage_rows, d), F32), pltpu.VMEM((bm, d), F32),
                            pltpu.SemaphoreType.DMA((2,)), pltpu.SemaphoreType.DMA],
        ),
        compiler_params=pltpu.CompilerParams(
            dimension_semantics=("arbitrary",), vmem_limit_bytes=VMEM_LIMIT_BYTES),
        name="dispatch",
    )(*runs, total, ldest, x2d)


def _expert_kernel(blk_ref, exp_ref, lo_ref, hi_ref, first_ref, xs_ref, wgu_ref, bgu_ref, wdn_ref, bdn_ref, ys_ref):
    i = pl.program_id(0)
    bm = xs_ref.shape[0]
    f = wdn_ref.shape[0]
    lo, hi = lo_ref[i], hi_ref[i]

    @pl.when(hi > lo)
    def _():
        gu = _dot(xs_ref[...].astype(BF16), wgu_ref[...]) + bgu_ref[...]
        gate = jnp.minimum(gu[:, :f], SWIGLU_LIMIT)
        up = jnp.clip(gu[:, f:], -SWIGLU_LIMIT, SWIGLU_LIMIT)
        act = (up + 1.0) * (gate * jax.nn.sigmoid(gate * SWIGLU_ALPHA))
        y = _dot(act.astype(BF16), wdn_ref[...]) + bdn_ref[...]
        row = lax.broadcasted_iota(jnp.int32, (bm, 1), 0)
        y = jnp.where(jnp.logical_and(row >= lo, row < hi), y, 0.0)

        @pl.when(first_ref[i] == 1)
        def _():
            ys_ref[...] = y

        @pl.when(first_ref[i] == 0)
        def _():
            ys_ref[...] = ys_ref[...] + y

    @pl.when(hi <= lo)
    def _():
        ys_ref[...] = jnp.zeros(ys_ref.shape, F32)


def _expert_call(l, items, xs, wgu, bgu, wdn, bdn, *, bm):
    n_rows, d = xs.shape
    f = wdn.shape[2]
    n_items = items[0].shape[0]
    assert n_rows % bm == 0
    return pl.pallas_call(
        _expert_kernel,
        out_shape=jax.ShapeDtypeStruct((n_rows, d), F32),
        grid_spec=pltpu.PrefetchScalarGridSpec(
            num_scalar_prefetch=5,
            grid=(n_items,),
            in_specs=[
                pl.BlockSpec((bm, d), lambda i, blk, ex, lo, hi, fi: (blk[i], 0)),
                pl.BlockSpec((None, None, d, 2 * f), lambda i, blk, ex, lo, hi, fi: (l, ex[i], 0, 0)),
                pl.BlockSpec((None, None, 1, 2 * f), lambda i, blk, ex, lo, hi, fi: (l, ex[i], 0, 0)),
                pl.BlockSpec((None, None, f, d), lambda i, blk, ex, lo, hi, fi: (l, ex[i], 0, 0)),
                pl.BlockSpec((None, None, 1, d), lambda i, blk, ex, lo, hi, fi: (l, ex[i], 0, 0)),
            ],
            out_specs=pl.BlockSpec((bm, d), lambda i, blk, ex, lo, hi, fi: (blk[i], 0)),
        ),
        compiler_params=pltpu.CompilerParams(
            dimension_semantics=("arbitrary",), vmem_limit_bytes=VMEM_LIMIT_BYTES),
        name="experts",
    )(*items, xs, wgu, bgu, wdn, bdn)


def _combine_kernel(nc_ref, ls_ref, gs_ref, chunks_ref, ld_ref, gt_ref, x_ref, l3g_ref, l3b_ref, ys_hbm, o_ref,
                    stage, sems, *, alpha, n_exp):
    s = pl.program_id(0)
    last = pl.num_programs(0) - 1
    tb, d = x_ref.shape
    n_rows = stage.shape[1]
    slot = s % 2

    def chunk_copy(slot_):
        def make(local_row, global_row):
            return pltpu.make_async_copy(ys_hbm.at[pl.ds(global_row, SUBLANES)],
                                         stage.at[slot_, pl.ds(local_row, SUBLANES)], sems.at[slot_])
        return make

    @pl.when(s == 0)
    def _():
        stage[...] = jnp.zeros(stage.shape, F32)
        _run_copies(nc_ref, ls_ref, gs_ref, s, n_exp, chunk_copy(slot), lambda c: c.start())

    @pl.when(s < last)
    def _():
        _run_copies(nc_ref, ls_ref, gs_ref, s + 1, n_exp, chunk_copy(1 - slot), lambda c: c.start())

    _wait_chunks(chunks_ref[s], chunk_copy(slot))

    pos = jnp.transpose(ld_ref[...].astype(F32))
    gates = jnp.transpose(gt_ref[...])
    rid = lax.broadcasted_iota(jnp.int32, (tb, n_rows), 1).astype(F32)
    weights = jnp.zeros((tb, n_rows), F32)
    for kk in range(TOP_K):
        weights = jnp.where(rid == pos[:, kk:kk + 1], gates[:, kk:kk + 1], weights)
    ff = _dot(weights.astype(BF16), stage[slot].astype(BF16))
    o_ref[...] = _layer_norm(alpha * x_ref[...] + ff, l3g_ref[...], l3b_ref[...])


def _combine_call(l, runs, ldest, gates, x2d, l3g, l3b, ys, *, alpha, n_exp, stage_rows, tb):
    n_tok, d = x2d.shape
    assert n_tok % tb == 0 and tb % LANES == 0
    slot = pl.BlockSpec((SUBLANES, tb), lambda i, *_: (0, i))
    vec = pl.BlockSpec((None, 1, d), lambda i, *_: (l, 0, 0))
    return pl.pallas_call(
        functools.partial(_combine_kernel, alpha=alpha, n_exp=n_exp),
        out_shape=jax.ShapeDtypeStruct((n_tok, d), F32),
        grid_spec=pltpu.PrefetchScalarGridSpec(
            num_scalar_prefetch=4,
            grid=(n_tok // tb,),
            in_specs=[slot, slot, pl.BlockSpec((tb, d), lambda i, *_: (i, 0)), vec, vec,
                      pl.BlockSpec(memory_space=pl.ANY)],
            out_specs=pl.BlockSpec((tb, d), lambda i, *_: (i, 0)),
            scratch_shapes=[pltpu.VMEM((2, stage_rows, d), F32), pltpu.SemaphoreType.DMA((2,))],
        ),
        compiler_params=pltpu.CompilerParams(
            dimension_semantics=("arbitrary",), vmem_limit_bytes=VMEM_LIMIT_BYTES),
        name="combine",
    )(*runs, ldest, gates, x2d, l3g, l3b, ys)


def kernel(x, mem, mem_ln_g, mem_ln_b, w_in, b_in, conv_a_w, conv_a_b, ln_a_g, ln_a_b, w_a_out, b_a_out,
           conv_b_w, w_b_out, pool_w, pool_scale, w_mix_out, ln1_g, ln1_b, w_xq, w_xk, w_xv, w_xo, ln2_g, ln2_b,
           router_w, router_b, w_gu, b_gu, w_down, b_down, ln3_g, ln3_b):
    bsz, t_len, d = x.shape
    depth = w_in.shape[0]
    n_exp = router_w.shape[2]
    n_tok = bsz * t_len
    alpha = float((2 * depth) ** 0.25)
    tile = min(256, t_len)
    bm = min(512, n_tok * TOP_K)
    n_tiles = n_tok // tile
    stage_rows = _round_up(TOP_K * tile + n_exp * (SUBLANES - 1), 2 * LANES)
    n_blocks = pl.cdiv(TOP_K * n_tok + n_tiles * n_exp * (SUBLANES - 1), bm)
    n_rows_out = (n_blocks + 1) * bm

    row = lambda a: a[:, None, :]
    bf = lambda a: a.astype(BF16)
    w_in_b, wao_b, wbo_b, pw_b, wmo_b = bf(w_in), bf(w_a_out), bf(w_b_out), bf(pool_w), bf(w_mix_out)
    wq_b, wo_b = bf(w_xq), bf(w_xo)
    rwt_b = bf(jnp.swapaxes(router_w, 1, 2))
    rb_c = router_b[:, :, None]
    wgu_b, wdn_b = bf(w_gu), bf(w_down)
    bgu_r, bdn_r = b_gu[:, :, None, :], b_down[:, :, None, :]

    k_all, v_all = _kv_call(mem, mem_ln_g[None, :], mem_ln_b[None, :], bf(w_xk), bf(w_xv))

    for l in range(depth):
        x1 = _mixer_call(l, x, w_in_b, row(b_in), conv_a_w, row(conv_a_b), row(ln_a_g), row(ln_a_b), wao_b,
                         row(b_a_out), conv_b_w, wbo_b, pw_b, row(pool_scale), wmo_b, row(ln1_g), row(ln1_b),
                         alpha=alpha, tb=tile)
        x2, gates, ldest, cp = _xattn_call(l, x1, k_all, v_all, wq_b, wo_b, row(ln2_g), row(ln2_b),
                                           rwt_b, rb_c, alpha=alpha, tb=tile)
        runs, total, items = _routing_plan(cp[:, :, 0].astype(jnp.int32), n_blocks, bm)
        x2d = x2.reshape(n_tok, d)
        xs = _dispatch_call(runs, total, ldest, x2d, n_exp=n_exp, n_rows_out=n_rows_out, stage_rows=stage_rows,
                            tb=tile, bm=bm)
        ys = _expert_call(l, items, xs, wgu_b, bgu_r, wdn_b, bdn_r, bm=bm)
        x = _combine_call(l, runs, ldest, gates, x2d, row(ln3_g), row(ln3_b), ys, alpha=alpha, n_exp=n_exp,
                          stage_rows=stage_rows, tb=tile).reshape(bsz, t_len, d)
    return x
```

```python
import functools

import jax
import jax.numpy as jnp
from jax import lax
from jax.experimental import pallas as pl
from jax.experimental.pallas import tpu as pltpu

N_XHEADS = 4
TOP_K = 4
POOL_WINDOWS = (2, 4, 8, 16)
SWIGLU_LIMIT = 7.0
SWIGLU_ALPHA = 1.702
LN_EPS = 1e-5

SUBLANES = 8
LANES = 128
VMEM_LIMIT_BYTES = 56 * 1024 * 1024

BF16 = jnp.bfloat16
F32 = jnp.float32


def _round_up(n, m):
    return (n + m - 1) // m * m


def _layer_norm(v, g, b):
    mu = jnp.mean(v, axis=-1, keepdims=True)
    c = v - mu
    var = jnp.mean(c * c, axis=-1, keepdims=True)
    return c * lax.rsqrt(var + LN_EPS) * g + b


def _dot(a, b):
    return jnp.dot(a, b, preferred_element_type=F32)


def _resident(block_shape, index_map):
    return pl.BlockSpec(block_shape, index_map, pipeline_mode=pl.Buffered(1))


def _kv_kernel(mem_ref, g_ref, b_ref, wk_ref, wv_ref, k_ref, v_ref):
    mn = _layer_norm(mem_ref[...], g_ref[...], b_ref[...]).astype(BF16)
    k_ref[...] = _dot(mn, wk_ref[...]).astype(BF16)
    v_ref[...] = _dot(mn, wv_ref[...]).astype(BF16)


def _kv_call(mem, g, b, wk, wv):
    bsz, m_len, d = mem.shape
    depth = wk.shape[0]
    out = jax.ShapeDtypeStruct((depth, bsz, m_len, d), BF16)
    return pl.pallas_call(
        _kv_kernel,
        out_shape=(out, out),
        grid=(depth, bsz),
        in_specs=[
            pl.BlockSpec((None, m_len, d), lambda l, i: (i, 0, 0)),
            pl.BlockSpec((1, d), lambda l, i: (0, 0)),
            pl.BlockSpec((1, d), lambda l, i: (0, 0)),
            pl.BlockSpec((None, d, d), lambda l, i: (l, 0, 0)),
            pl.BlockSpec((None, d, d), lambda l, i: (l, 0, 0)),
        ],
        out_specs=(
            pl.BlockSpec((None, None, m_len, d), lambda l, i: (l, i, 0, 0)),
            pl.BlockSpec((None, None, m_len, d), lambda l, i: (l, i, 0, 0)),
        ),
        compiler_params=pltpu.CompilerParams(
            dimension_semantics=("arbitrary", "arbitrary"), vmem_limit_bytes=VMEM_LIMIT_BYTES),
        name="mem_kv",
    )(mem, g, b, wk, wv)


def _causal_taps(buf, shifted, w_ref, halo, tb, row_chunk, out_ref, between=()):
    k_width = w_ref.shape[0]
    d = buf.shape[1]
    first = halo - (k_width - 1)
    n_rows = halo + tb - SUBLANES
    residues = sorted({o % SUBLANES for o in range(first, halo + 1)} - {0})
    for res in residues:
        shifted[res - 1, 0:n_rows, :] = buf[res:res + n_rows, :]
    between = list(between)
    for c0 in range(0, d, LANES):
        cols = slice(c0, c0 + LANES)
        if c0 > 0 and between:
            between.pop(0)()
        for r0 in range(0, tb, row_chunk):
            acc = None
            for o in range(first, halo + 1):
                res = o % SUBLANES
                start = r0 + o - res
                rows = buf[start:start + row_chunk, cols] if res == 0 else shifted[res - 1, start:start + row_chunk, cols]
                term = w_ref[o - first:o - first + 1, cols] * rows
                acc = term if acc is None else acc + term
            out_ref[r0:r0 + row_chunk, cols] = acc
    for job in between:
        job()


def _mixer_kernel(x_ref, w_in_ref, b_in_ref, caw_ref, cab_ref, lag_ref, lab_ref, wao_ref, bao_ref,
                  cbw_ref, wbo_ref, pw_ref, ps_ref, wmo_ref, l1g_ref, l1b_ref, o_ref,
                  abuf, bbuf, cbuf, tmp, shifted, *, alpha, halo_a, halo_b, halo_c):
    tb, d = x_ref.shape
    t = pl.program_id(1)

    @pl.when(t == 0)
    def _():
        abuf[0:halo_a, :] = jnp.zeros((halo_a, d), F32)
        bbuf[0:halo_b, :] = jnp.zeros((halo_b, d), F32)
        cbuf[0:halo_c, :] = jnp.zeros((halo_c, d), F32)

    x = x_ref[...]
    xb = x.astype(BF16)

    def proj(j):
        return _dot(xb, w_in_ref[:, j * d:(j + 1) * d]) + b_in_ref[:, j * d:(j + 1) * d]

    held = {}

    def stage_b():
        bbuf[halo_b:halo_b + tb, :] = proj(3) * proj(4)

    def stage_c():
        cbuf[halo_c:halo_c + tb, :] = proj(5)

    def hold(name, j, fn):
        def job():
            held[name] = fn(proj(j))
        return job

    jobs = [stage_b, hold("gate_b", 2, lambda v: v), stage_c, hold("mix_a", 6, jax.nn.sigmoid),
            hold("mix_b", 7, jax.nn.sigmoid), hold("mix_c", 8, jax.nn.sigmoid)]

    abuf[halo_a:halo_a + tb, :] = proj(0) * jax.nn.sigmoid(proj(1))
    _causal_taps(abuf, shifted, caw_ref, halo_a, tb, 128, tmp, between=jobs)
    a = _layer_norm(tmp[...] + cab_ref[...], lag_ref[...], lab_ref[...])
    a = a * jax.nn.sigmoid(a)
    y_a = _dot(a.astype(BF16), wao_ref[...]) + bao_ref[...]
    m = held["mix_a"] * y_a
    abuf[0:halo_a, :] = abuf[tb:tb + halo_a, :]

    kb = cbw_ref.shape[0]
    u_b = None
    for k in range(kb):
        term = cbw_ref[k:k + 1, :] * bbuf[halo_b - (kb - 1) + k:halo_b - (kb - 1) + k + tb, :]
        u_b = term if u_b is None else u_b + term
    y_b = _dot((held["gate_b"] * u_b).astype(BF16), wbo_ref[...])
    m = m + held["mix_b"] * y_b
    bbuf[0:halo_b, :] = bbuf[tb:tb + halo_b, :]

    n_groups = len(POOL_WINDOWS)
    pg = d // n_groups
    pos = t * tb + lax.broadcasted_iota(jnp.int32, (tb, 1), 0)
    y_parts = []
    for gi, win in enumerate(POOL_WINDOWS):
        lo = gi * pg
        u = cbuf[halo_c:halo_c + tb, lo:lo + pg]
        s = u
        for j in range(1, win):
            s = s + cbuf[halo_c - j:halo_c - j + tb, lo:lo + pg]
        cnt = jnp.minimum(pos + 1, win).astype(F32)
        pooled = s / cnt - u
        y_parts.append(_dot(pooled.astype(BF16), pw_ref[gi]))
    y_c = jnp.concatenate(y_parts, axis=-1) * ps_ref[...]
    m = m + held["mix_c"] * y_c
    cbuf[0:halo_c, :] = cbuf[tb:tb + halo_c, :]

    mix = _dot(m.astype(BF16), wmo_ref[...])
    o_ref[...] = _layer_norm(alpha * x + mix, l1g_ref[...], l1b_ref[...])


def _mixer_call(l, x, w_in, b_in, caw, cab, lag, lab, wao, bao, cbw, wbo, pw, ps, wmo, l1g, l1b, *, alpha, tb):
    bsz, t_len, d = x.shape
    d_in = w_in.shape[2]
    ka, kb = caw.shape[1], cbw.shape[1]
    n_groups, pg = pw.shape[1], pw.shape[2]
    halo_a = _round_up(ka - 1, SUBLANES)
    halo_b = _round_up(kb - 1, SUBLANES)
    halo_c = _round_up(max(POOL_WINDOWS) - 1, SUBLANES)
    assert t_len % tb == 0 and tb % 128 == 0 and tb >= halo_a
    vec = lambda n: pl.BlockSpec((None, 1, n), lambda i, t: (l, 0, 0))
    kern = functools.partial(_mixer_kernel, alpha=alpha, halo_a=halo_a, halo_b=halo_b, halo_c=halo_c)
    return pl.pallas_call(
        kern,
        out_shape=jax.ShapeDtypeStruct((bsz, t_len, d), F32),
        grid=(bsz, t_len // tb),
        in_specs=[
            pl.BlockSpec((None, tb, d), lambda i, t: (i, t, 0)),
            _resident((None, d, d_in), lambda i, t: (l, 0, 0)),
            vec(d_in),
            pl.BlockSpec((None, ka, d), lambda i, t: (l, 0, 0)),
            vec(d), vec(d), vec(d),
            _resident((None, d, d), lambda i, t: (l, 0, 0)),
            vec(d),
            pl.BlockSpec((None, kb, d), lambda i, t: (l, 0, 0)),
            _resident((None, d, d), lambda i, t: (l, 0, 0)),
            _resident((None, n_groups, pg, pg), lambda i, t: (l, 0, 0, 0)),
            vec(d),
            _resident((None, d, d), lambda i, t: (l, 0, 0)),
            vec(d), vec(d),
        ],
        out_specs=pl.BlockSpec((None, tb, d), lambda i, t: (i, t, 0)),
        scratch_shapes=[
            pltpu.VMEM((halo_a + tb, d), F32),
            pltpu.VMEM((halo_b + tb, d), F32),
            pltpu.VMEM((halo_c + tb, d), F32),
            pltpu.VMEM((tb, d), F32),
            pltpu.VMEM((SUBLANES - 1, halo_a + tb - SUBLANES, d), F32),
        ],
        compiler_params=pltpu.CompilerParams(
            dimension_semantics=("arbitrary", "arbitrary"), vmem_limit_bytes=VMEM_LIMIT_BYTES),
        name="mixer",
    )(x, w_in, b_in, caw, cab, lag, lab, wao, bao, cbw, wbo, pw, ps, wmo, l1g, l1b)


def _xattn_kernel(x_ref, k_ref, v_ref, wq_ref, wo_ref, l2g_ref, l2b_ref, rwt_ref, rb_ref,
                  o_ref, gt_ref, ld_ref, cp_ref, *, alpha):
    tb, d = x_ref.shape
    n_exp = rwt_ref.shape[0]
    dh = d // N_XHEADS
    scale = dh ** -0.5

    x = x_ref[...]
    q = _dot(x.astype(BF16), wq_ref[...])
    heads = []
    for h in range(N_XHEADS):
        qh = q[:, h * dh:(h + 1) * dh].astype(BF16)
        kh = k_ref[:, h * dh:(h + 1) * dh]
        vh = v_ref[:, h * dh:(h + 1) * dh]
        s = lax.dot_general(qh, kh, (((1,), (1,)), ((), ())), preferred_element_type=F32) * scale
        p = jnp.exp(s - jnp.max(s, axis=-1, keepdims=True))
        denom = jnp.sum(p, axis=-1, keepdims=True)
        heads.append(_dot(p.astype(BF16), vh) / denom)
    att = _dot(jnp.concatenate(heads, axis=-1).astype(BF16), wo_ref[...])
    x2 = _layer_norm(alpha * x + att, l2g_ref[...], l2b_ref[...])
    o_ref[...] = x2

    logits = lax.dot_general(rwt_ref[...], x2.astype(BF16), (((1,), (1,)), ((), ())),
                             preferred_element_type=F32) + rb_ref[...]
    eidx = lax.broadcasted_iota(jnp.int32, (n_exp, tb), 0).astype(F32)
    work = logits
    tops, args = [], []
    member = jnp.zeros((n_exp, tb), F32)
    for _ in range(TOP_K):
        mx = jnp.max(work, axis=0, keepdims=True)
        am = jnp.min(jnp.where(work == mx, eidx, float(n_exp)), axis=0, keepdims=True)
        hit = eidx == am
        tops.append(mx)
        args.append(am)
        member = jnp.where(hit, 1.0, member)
        work = jnp.where(hit, -jnp.inf, work)
    exps = [jnp.exp(tv - tops[0]) for tv in tops]
    esum = exps[0]
    for ev in exps[1:]:
        esum = esum + ev

    before = lax.broadcasted_iota(jnp.int32, (tb, tb), 0) < lax.broadcasted_iota(jnp.int32, (tb, tb), 1)
    prefix = _dot(member.astype(BF16), jnp.where(before, 1.0, 0.0).astype(BF16))
    count = jnp.sum(member, axis=1, keepdims=True)
    padded = jnp.floor((count + (SUBLANES - 1)) * (1.0 / SUBLANES)) * SUBLANES
    padded = jnp.broadcast_to(padded, (n_exp, LANES))
    lower = lax.broadcasted_iota(jnp.int32, (n_exp, n_exp), 1) < lax.broadcasted_iota(jnp.int32, (n_exp, n_exp), 0)
    run_start = _dot(jnp.where(lower, 1.0, 0.0).astype(BF16), padded.astype(BF16))
    position = prefix + run_start[:, 0:1]
    gt_ref[...] = jnp.zeros(gt_ref.shape, F32)
    ld_ref[...] = jnp.zeros(ld_ref.shape, jnp.int32)
    for kk in range(TOP_K):
        gt_ref[kk:kk + 1, :] = exps[kk] / esum
        pos = jnp.sum(jnp.where(eidx == args[kk], position, 0.0), axis=0, keepdims=True)
        ld_ref[kk:kk + 1, :] = pos.astype(jnp.int32)
    cp_ref[...] = padded


def _xattn_call(l, x, k_all, v_all, wq, wo, l2g, l2b, rwt, rb, *, alpha, tb):
    bsz, t_len, d = x.shape
    m_len = k_all.shape[2]
    n_exp = rwt.shape[1]
    n_tok = bsz * t_len
    nt = t_len // tb
    assert t_len % tb == 0 and tb % LANES == 0
    vec = lambda n: pl.BlockSpec((None, 1, n), lambda i, t: (l, 0, 0))
    slot = pl.BlockSpec((SUBLANES, tb), lambda i, t: (0, i * nt + t))
    return pl.pallas_call(
        functools.partial(_xattn_kernel, alpha=alpha),
        out_shape=(
            jax.ShapeDtypeStruct((bsz, t_len, d), F32),
            jax.ShapeDtypeStruct((SUBLANES, n_tok), F32),
            jax.ShapeDtypeStruct((SUBLANES, n_tok), jnp.int32),
            jax.ShapeDtypeStruct((n_tok // tb, n_exp, LANES), F32),
        ),
        grid=(bsz, nt),
        in_specs=[
            pl.BlockSpec((None, tb, d), lambda i, t: (i, t, 0)),
            pl.BlockSpec((None, None, m_len, d), lambda i, t: (l, i, 0, 0)),
            pl.BlockSpec((None, None, m_len, d), lambda i, t: (l, i, 0, 0)),
            _resident((None, d, d), lambda i, t: (l, 0, 0)),
            _resident((None, d, d), lambda i, t: (l, 0, 0)),
            vec(d), vec(d),
            pl.BlockSpec((None, n_exp, d), lambda i, t: (l, 0, 0)),
            pl.BlockSpec((None, n_exp, 1), lambda i, t: (l, 0, 0)),
        ],
        out_specs=(
            pl.BlockSpec((None, tb, d), lambda i, t: (i, t, 0)),
            slot, slot,
            pl.BlockSpec((None, n_exp, LANES), lambda i, t: (i * nt + t, 0, 0)),
        ),
        compiler_params=pltpu.CompilerParams(
            dimension_semantics=("arbitrary", "arbitrary"), vmem_limit_bytes=VMEM_LIMIT_BYTES),
        name="xattn_router",
    )(x, k_all, v_all, wq, wo, l2g, l2b, rwt, rb)


def _routing_plan(cp, n_blocks, bm):
    n_exp = cp.shape[1]
    local_start = jnp.cumsum(cp, axis=1) - cp
    tot = jnp.sum(cp, axis=0)
    gend = jnp.cumsum(tot)
    gstart = gend - tot
    global_start = gstart[None, :] + jnp.cumsum(cp, axis=0) - cp
    total = gend[-1]
    tile_chunks = jnp.sum(cp, axis=1) // SUBLANES

    n_items = n_blocks + n_exp + 1
    first_blk = gstart // bm
    nblk = jnp.where(tot > 0, (gend + bm - 1) // bm - first_blk, 0)
    item_end = jnp.cumsum(nblk)
    item_start = item_end - nblk
    it = jnp.arange(n_items, dtype=jnp.int32)
    e_raw = jnp.sum(item_end[None, :] <= it[:, None], axis=1).astype(jnp.int32)
    valid = e_raw < n_exp
    e = jnp.minimum(e_raw, n_exp - 1)
    unused_blk = jnp.minimum((total - 1) // bm + 1 + it - item_end[-1], n_blocks)
    blk = jnp.where(valid, first_blk[e] + it - item_start[e], unused_blk)
    lo = jnp.where(valid, jnp.maximum(gstart[e], blk * bm) - blk * bm, 0)
    hi = jnp.where(valid, jnp.minimum(gend[e], (blk + 1) * bm) - blk * bm, 0)
    first = jnp.logical_or(jnp.logical_not(valid), lo == 0)
    i32 = lambda a: a.astype(jnp.int32)
    runs = tuple(i32(a.reshape(-1)) for a in (cp // SUBLANES, local_start, global_start)) + (i32(tile_chunks),)
    return runs, i32(total.reshape(1)), tuple(map(i32, (blk, e, lo, hi, first)))


def _run_copies(nc_ref, ls_ref, gs_ref, step, n_exp, max_chunks, make_copy):
    def per_expert(e, carry):
        idx = step * n_exp + e
        n, local0, global0 = nc_ref[idx], ls_ref[idx], gs_ref[idx]
        for b in range(max_chunks.bit_length()):
            @pl.when(((n >> b) & 1) == 1)
            def _():
                off = ((n >> (b + 1)) << (b + 1)) * SUBLANES
                make_copy(pl.multiple_of(local0 + off, SUBLANES), pl.multiple_of(global0 + off, SUBLANES),
                          SUBLANES << b).start()
        return carry

    lax.fori_loop(0, n_exp, per_expert, 0)


def _wait_chunks(n_chunks, max_chunks, make_copy):
    for b in range(max_chunks.bit_length()):
        @pl.when(((n_chunks >> b) & 1) == 1)
        def _():
            make_copy(0, 0, SUBLANES << b).wait()


def _dispatch_kernel(nc_ref, ls_ref, gs_ref, chunks_ref, total_ref, ld_ref, x_ref, xs_hbm, stage, zeros, sems, zsem,
                     *, n_exp):
    s = pl.program_id(0)
    last = pl.num_programs(0) - 1
    tb, d = x_ref.shape
    n_rows = stage.shape[1]
    slot = s % 2

    run_chunks, tile_chunks = tb // SUBLANES, n_rows // SUBLANES

    def chunk_copy(slot_):
        def make(local_row, global_row, rows):
            return pltpu.make_async_copy(stage.at[slot_, pl.ds(local_row, rows)],
                                         xs_hbm.at[pl.ds(global_row, rows)], sems.at[slot_])
        return make

    rid = lax.broadcasted_iota(jnp.int32, (n_rows, tb), 0)
    onehot = jnp.zeros((n_rows, tb), F32)
    for kk in range(TOP_K):
        onehot = jnp.where(rid == ld_ref[kk:kk + 1, :], 1.0, onehot)
    stage[slot] = _dot(onehot.astype(BF16), x_ref[...].astype(BF16))

    _run_copies(nc_ref, ls_ref, gs_ref, s, n_exp, run_chunks, chunk_copy(slot))

    @pl.when(s > 0)
    def _():
        _wait_chunks(chunks_ref[s - 1], tile_chunks, chunk_copy(1 - slot))

    @pl.when(s == last)
    def _():
        _wait_chunks(chunks_ref[s], tile_chunks, chunk_copy(slot))
        zeros[...] = jnp.zeros(zeros.shape, F32)
        zrows = zeros.shape[0]
        total = total_ref[0]
        rest = xs_hbm.shape[0] - total
        n_big = rest // zrows
        tail0 = total + n_big * zrows

        def big(j):
            return pltpu.make_async_copy(
                zeros, xs_hbm.at[pl.ds(pl.multiple_of(total + j * zrows, SUBLANES), zrows)], zsem)

        def small(j):
            return pltpu.make_async_copy(
                zeros.at[pl.ds(0, SUBLANES)],
                xs_hbm.at[pl.ds(pl.multiple_of(tail0 + j * SUBLANES, SUBLANES), SUBLANES)], zsem)

        for make, count in ((big, n_big), (small, (rest - n_big * zrows) // SUBLANES)):
            lax.fori_loop(0, count, lambda j, c, make=make: (make(j).start(), c)[1], 0)
            lax.fori_loop(0, count, lambda j, c, make=make: (make(j).wait(), c)[1], 0)


def _dispatch_call(runs, total, ldest, x2d, *, n_exp, n_rows_out, stage_rows, tb, bm):
    n_tok, d = x2d.shape
    assert n_tok % tb == 0
    return pl.pallas_call(
        functools.partial(_dispatch_kernel, n_exp=n_exp),
        out_shape=jax.ShapeDtypeStruct((n_rows_out, d), F32),
        grid_spec=pltpu.PrefetchScalarGridSpec(
            num_scalar_prefetch=5,
            grid=(n_tok // tb,),
            in_specs=[pl.BlockSpec((SUBLANES, tb), lambda i, *_: (0, i)),
                      pl.BlockSpec((tb, d), lambda i, *_: (i, 0))],
            out_specs=pl.BlockSpec(memory_space=pl.ANY),
            scratch_shapes=[pltpu.VMEM((2, stage_rows, d), F32), pltpu.VMEM((bm, d), F32),
                            pltpu.SemaphoreType.DMA((2,)), pltpu.SemaphoreType.DMA],
        ),
        compiler_params=pltpu.CompilerParams(
            dimension_semantics=("arbitrary",), vmem_limit_bytes=VMEM_LIMIT_BYTES),
        name="dispatch",
    )(*runs, total, ldest, x2d)


def _expert_kernel(blk_ref, exp_ref, lo_ref, hi_ref, first_ref, xs_ref, wgu_ref, bgu_ref, wdn_ref, bdn_ref, ys_ref):
    i = pl.program_id(0)
    bm = xs_ref.shape[0]
    f = wdn_ref.shape[0]
    lo, hi = lo_ref[i], hi_ref[i]

    @pl.when(hi > lo)
    def _():
        gu = _dot(xs_ref[...].astype(BF16), wgu_ref[...]) + bgu_ref[...]
        gate = jnp.minimum(gu[:, :f], SWIGLU_LIMIT)
        up = jnp.clip(gu[:, f:], -SWIGLU_LIMIT, SWIGLU_LIMIT)
        act = (up + 1.0) * (gate * jax.nn.sigmoid(gate * SWIGLU_ALPHA))
        y = _dot(act.astype(BF16), wdn_ref[...]) + bdn_ref[...]
        row = lax.broadcasted_iota(jnp.int32, (bm, 1), 0)
        y = jnp.where(jnp.logical_and(row >= lo, row < hi), y, 0.0)

        @pl.when(first_ref[i] == 1)
        def _():
            ys_ref[...] = y

        @pl.when(first_ref[i] == 0)
        def _():
            ys_ref[...] = ys_ref[...] + y

    @pl.when(hi <= lo)
    def _():
        ys_ref[...] = jnp.zeros(ys_ref.shape, F32)


def _expert_call(l, items, xs, wgu, bgu, wdn, bdn, *, bm):
    n_rows, d = xs.shape
    f = wdn.shape[2]
    n_items = items[0].shape[0]
    assert n_rows % bm == 0
    return pl.pallas_call(
        _expert_kernel,
        out_shape=jax.ShapeDtypeStruct((n_rows, d), F32),
        grid_spec=pltpu.PrefetchScalarGridSpec(
            num_scalar_prefetch=5,
            grid=(n_items,),
            in_specs=[
                pl.BlockSpec((bm, d), lambda i, blk, ex, lo, hi, fi: (blk[i], 0)),
                pl.BlockSpec((None, None, d, 2 * f), lambda i, blk, ex, lo, hi, fi: (l, ex[i], 0, 0)),
                pl.BlockSpec((None, None, 1, 2 * f), lambda i, blk, ex, lo, hi, fi: (l, ex[i], 0, 0)),
                pl.BlockSpec((None, None, f, d), lambda i, blk, ex, lo, hi, fi: (l, ex[i], 0, 0)),
                pl.BlockSpec((None, None, 1, d), lambda i, blk, ex, lo, hi, fi: (l, ex[i], 0, 0)),
            ],
            out_specs=pl.BlockSpec((bm, d), lambda i, blk, ex, lo, hi, fi: (blk[i], 0)),
        ),
        compiler_params=pltpu.CompilerParams(
            dimension_semantics=("arbitrary",), vmem_limit_bytes=VMEM_LIMIT_BYTES),
        name="experts",
    )(*items, xs, wgu, bgu, wdn, bdn)


def _combine_kernel(nc_ref, ls_ref, gs_ref, chunks_ref, ld_ref, gt_ref, x_ref, l3g_ref, l3b_ref, ys_hbm, o_ref,
                    stage, sems, *, alpha, n_exp):
    s = pl.program_id(0)
    last = pl.num_programs(0) - 1
    tb, d = x_ref.shape
    n_rows = stage.shape[1]
    slot = s % 2

    run_chunks, tile_chunks = tb // SUBLANES, n_rows // SUBLANES

    def chunk_copy(slot_):
        def make(local_row, global_row, rows):
            return pltpu.make_async_copy(ys_hbm.at[pl.ds(global_row, rows)],
                                         stage.at[slot_, pl.ds(local_row, rows)], sems.at[slot_])
        return make

    @pl.when(s == 0)
    def _():
        stage[...] = jnp.zeros(stage.shape, F32)
        _run_copies(nc_ref, ls_ref, gs_ref, s, n_exp, run_chunks, chunk_copy(slot))

    @pl.when(s < last)
    def _():
        _run_copies(nc_ref, ls_ref, gs_ref, s + 1, n_exp, run_chunks, chunk_copy(1 - slot))

    _wait_chunks(chunks_ref[s], tile_chunks, chunk_copy(slot))

    pos = jnp.transpose(ld_ref[...].astype(F32))
    gates = jnp.transpose(gt_ref[...])
    rid = lax.broadcasted_iota(jnp.int32, (tb, n_rows), 1).astype(F32)
    weights = jnp.zeros((tb, n_rows), F32)
    for kk in range(TOP_K):
        weights = jnp.where(rid == pos[:, kk:kk + 1], gates[:, kk:kk + 1], weights)
    ff = _dot(weights.astype(BF16), stage[slot].astype(BF16))
    o_ref[...] = _layer_norm(alpha * x_ref[...] + ff, l3g_ref[...], l3b_ref[...])


def _combine_call(l, runs, ldest, gates, x2d, l3g, l3b, ys, *, alpha, n_exp, stage_rows, tb):
    n_tok, d = x2d.shape
    assert n_tok % tb == 0 and tb % LANES == 0
    slot = pl.BlockSpec((SUBLANES, tb), lambda i, *_: (0, i))
    vec = pl.BlockSpec((None, 1, d), lambda i, *_: (l, 0, 0))
    return pl.pallas_call(
        functools.partial(_combine_kernel, alpha=alpha, n_exp=n_exp),
        out_shape=jax.ShapeDtypeStruct((n_tok, d), F32),
        grid_spec=pltpu.PrefetchScalarGridSpec(
            num_scalar_prefetch=4,
            grid=(n_tok // tb,),
            in_specs=[slot, slot, pl.BlockSpec((tb, d), lambda i, *_: (i, 0)), vec, vec,
                      pl.BlockSpec(memory_space=pl.ANY)],
            out_specs=pl.BlockSpec((tb, d), lambda i, *_: (i, 0)),
            scratch_shapes=[pltpu.VMEM((2, stage_rows, d), F32), pltpu.SemaphoreType.DMA((2,))],
        ),
        compiler_params=pltpu.CompilerParams(
            dimension_semantics=("arbitrary",), vmem_limit_bytes=VMEM_LIMIT_BYTES),
        name="combine",
    )(*runs, ldest, gates, x2d, l3g, l3b, ys)


def kernel(x, mem, mem_ln_g, mem_ln_b, w_in, b_in, conv_a_w, conv_a_b, ln_a_g, ln_a_b, w_a_out, b_a_out,
           conv_b_w, w_b_out, pool_w, pool_scale, w_mix_out, ln1_g, ln1_b, w_xq, w_xk, w_xv, w_xo, ln2_g, ln2_b,
           router_w, router_b, w_gu, b_gu, w_down, b_down, ln3_g, ln3_b):
    bsz, t_len, d = x.shape
    depth = w_in.shape[0]
    n_exp = router_w.shape[2]
    n_tok = bsz * t_len
    alpha = float((2 * depth) ** 0.25)
    tile = min(256, t_len)
    rtile = min(512, t_len)
    bm = min(512, n_tok * TOP_K)
    n_tiles = n_tok // rtile
    stage_rows = _round_up(TOP_K * rtile + n_exp * (SUBLANES - 1), 2 * LANES)
    n_blocks = pl.cdiv(TOP_K * n_tok + n_tiles * n_exp * (SUBLANES - 1), bm)
    n_rows_out = (n_blocks + 1) * bm

    row = lambda a: a[:, None, :]
    bf = lambda a: a.astype(BF16)
    w_in_b, wao_b, wbo_b, pw_b, wmo_b = bf(w_in), bf(w_a_out), bf(w_b_out), bf(pool_w), bf(w_mix_out)
    wq_b, wo_b = bf(w_xq), bf(w_xo)
    rwt_b = bf(jnp.swapaxes(router_w, 1, 2))
    rb_c = router_b[:, :, None]
    wgu_b, wdn_b = bf(w_gu), bf(w_down)
    bgu_r, bdn_r = b_gu[:, :, None, :], b_down[:, :, None, :]

    k_all, v_all = _kv_call(mem, mem_ln_g[None, :], mem_ln_b[None, :], bf(w_xk), bf(w_xv))

    for l in range(depth):
        x1 = _mixer_call(l, x, w_in_b, row(b_in), conv_a_w, row(conv_a_b), row(ln_a_g), row(ln_a_b), wao_b,
                         row(b_a_out), conv_b_w, wbo_b, pw_b, row(pool_scale), wmo_b, row(ln1_g), row(ln1_b),
                         alpha=alpha, tb=tile)
        x2, gates, ldest, cp = _xattn_call(l, x1, k_all, v_all, wq_b, wo_b, row(ln2_g), row(ln2_b),
                                           rwt_b, rb_c, alpha=alpha, tb=rtile)
        runs, total, items = _routing_plan(cp[:, :, 0].astype(jnp.int32), n_blocks, bm)
        x2d = x2.reshape(n_tok, d)
        xs = _dispatch_call(runs, total, ldest, x2d, n_exp=n_exp, n_rows_out=n_rows_out, stage_rows=stage_rows,
                            tb=rtile, bm=bm)
        ys = _expert_call(l, items, xs, wgu_b, bgu_r, wdn_b, bdn_r, bm=bm)
        x = _combine_call(l, runs, ldest, gates, x2d, row(ln3_g), row(ln3_b), ys, alpha=alpha, n_exp=n_exp,
                          stage_rows=stage_rows, tb=rtile).reshape(bsz, t_len, d)
    return x
```

```python
import functools

import jax
import jax.numpy as jnp
from jax import lax
from jax.experimental import pallas as pl
from jax.experimental.pallas import tpu as pltpu

N_XHEADS = 4
TOP_K = 4
POOL_WINDOWS = (2, 4, 8, 16)
SWIGLU_LIMIT = 7.0
SWIGLU_ALPHA = 1.702
LN_EPS = 1e-5

SUBLANES = 8
LANES = 128
VMEM_LIMIT_BYTES = 56 * 1024 * 1024

BF16 = jnp.bfloat16
F32 = jnp.float32


def _round_up(n, m):
    return (n + m - 1) // m * m


def _layer_norm(v, g, b):
    mu = jnp.mean(v, axis=-1, keepdims=True)
    c = v - mu
    var = jnp.mean(c * c, axis=-1, keepdims=True)
    return c * lax.rsqrt(var + LN_EPS) * g + b


def _dot(a, b):
    return jnp.dot(a, b, preferred_element_type=F32)


def _resident(block_shape, index_map):
    return pl.BlockSpec(block_shape, index_map, pipeline_mode=pl.Buffered(1))


def _kv_kernel(mem_ref, g_ref, b_ref, wk_ref, wv_ref, k_ref, v_ref):
    mn = _layer_norm(mem_ref[...], g_ref[...], b_ref[...]).astype(BF16)
    k_ref[...] = _dot(mn, wk_ref[...]).astype(BF16)
    v_ref[...] = _dot(mn, wv_ref[...]).astype(BF16)


def _kv_call(mem, g, b, wk, wv):
    bsz, m_len, d = mem.shape
    depth = wk.shape[0]
    out = jax.ShapeDtypeStruct((depth, bsz, m_len, d), BF16)
    return pl.pallas_call(
        _kv_kernel,
        out_shape=(out, out),
        grid=(depth, bsz),
        in_specs=[
            pl.BlockSpec((None, m_len, d), lambda l, i: (i, 0, 0)),
            pl.BlockSpec((1, d), lambda l, i: (0, 0)),
            pl.BlockSpec((1, d), lambda l, i: (0, 0)),
            pl.BlockSpec((None, d, d), lambda l, i: (l, 0, 0)),
            pl.BlockSpec((None, d, d), lambda l, i: (l, 0, 0)),
        ],
        out_specs=(
            pl.BlockSpec((None, None, m_len, d), lambda l, i: (l, i, 0, 0)),
            pl.BlockSpec((None, None, m_len, d), lambda l, i: (l, i, 0, 0)),
        ),
        compiler_params=pltpu.CompilerParams(
            dimension_semantics=("arbitrary", "arbitrary"), vmem_limit_bytes=VMEM_LIMIT_BYTES),
        name="mem_kv",
    )(mem, g, b, wk, wv)


def _causal_taps(buf, shifted, w_ref, halo, tb, row_chunk, out_ref, between=()):
    k_width = w_ref.shape[0]
    d = buf.shape[1]
    first = halo - (k_width - 1)
    n_rows = halo + tb - SUBLANES
    residues = sorted({o % SUBLANES for o in range(first, halo + 1)} - {0})
    for res in residues:
        shifted[res - 1, 0:n_rows, :] = buf[res:res + n_rows, :]
    between = list(between)
    for c0 in range(0, d, LANES):
        cols = slice(c0, c0 + LANES)
        if c0 > 0 and between:
            between.pop(0)()
        for r0 in range(0, tb, row_chunk):
            acc = None
            for o in range(first, halo + 1):
                res = o % SUBLANES
                start = r0 + o - res
                rows = buf[start:start + row_chunk, cols] if res == 0 else shifted[res - 1, start:start + row_chunk, cols]
                term = w_ref[o - first:o - first + 1, cols] * rows
                acc = term if acc is None else acc + term
            out_ref[r0:r0 + row_chunk, cols] = acc
    for job in between:
        job()


def _mixer_kernel(x_ref, w_in_ref, b_in_ref, caw_ref, cab_ref, lag_ref, lab_ref, wao_ref, bao_ref,
                  cbw_ref, wbo_ref, pw_ref, ps_ref, wmo_ref, l1g_ref, l1b_ref, o_ref,
                  abuf, bbuf, cbuf, tmp, shifted, *, alpha, halo_a, halo_b, halo_c):
    tb, d = x_ref.shape
    t = pl.program_id(1)

    @pl.when(t == 0)
    def _():
        abuf[0:halo_a, :] = jnp.zeros((halo_a, d), F32)
        bbuf[0:halo_b, :] = jnp.zeros((halo_b, d), F32)
        cbuf[0:halo_c, :] = jnp.zeros((halo_c, d), F32)

    x = x_ref[...]
    xb = x.astype(BF16)

    def proj(j):
        return _dot(xb, w_in_ref[:, j * d:(j + 1) * d]) + b_in_ref[:, j * d:(j + 1) * d]

    held = {}

    def stage_b():
        bbuf[halo_b:halo_b + tb, :] = proj(3) * proj(4)

    def stage_c():
        cbuf[halo_c:halo_c + tb, :] = proj(5)

    def hold(name, j, fn):
        def job():
            held[name] = fn(proj(j))
        return job

    jobs = [stage_b, hold("gate_b", 2, lambda v: v), stage_c, hold("mix_a", 6, jax.nn.sigmoid),
            hold("mix_b", 7, jax.nn.sigmoid), hold("mix_c", 8, jax.nn.sigmoid)]

    abuf[halo_a:halo_a + tb, :] = proj(0) * jax.nn.sigmoid(proj(1))
    _causal_taps(abuf, shifted, caw_ref, halo_a, tb, 128, tmp, between=jobs)
    a = _layer_norm(tmp[...] + cab_ref[...], lag_ref[...], lab_ref[...])
    a = a * jax.nn.sigmoid(a)
    y_a = _dot(a.astype(BF16), wao_ref[...]) + bao_ref[...]
    m = held["mix_a"] * y_a
    abuf[0:halo_a, :] = abuf[tb:tb + halo_a, :]

    kb = cbw_ref.shape[0]
    u_b = None
    for k in range(kb):
        term = cbw_ref[k:k + 1, :] * bbuf[halo_b - (kb - 1) + k:halo_b - (kb - 1) + k + tb, :]
        u_b = term if u_b is None else u_b + term
    y_b = _dot((held["gate_b"] * u_b).astype(BF16), wbo_ref[...])
    m = m + held["mix_b"] * y_b
    bbuf[0:halo_b, :] = bbuf[tb:tb + halo_b, :]

    n_groups = len(POOL_WINDOWS)
    pg = d // n_groups
    pos = t * tb + lax.broadcasted_iota(jnp.int32, (tb, 1), 0)
    y_parts = []
    for gi, win in enumerate(POOL_WINDOWS):
        lo = gi * pg
        u = cbuf[halo_c:halo_c + tb, lo:lo + pg]
        s = u
        for j in range(1, win):
            s = s + cbuf[halo_c - j:halo_c - j + tb, lo:lo + pg]
        cnt = jnp.minimum(pos + 1, win).astype(F32)
        pooled = s / cnt - u
        y_parts.append(_dot(pooled.astype(BF16), pw_ref[gi]))
    y_c = jnp.concatenate(y_parts, axis=-1) * ps_ref[...]
    m = m + held["mix_c"] * y_c
    cbuf[0:halo_c, :] = cbuf[tb:tb + halo_c, :]

    mix = _dot(m.astype(BF16), wmo_ref[...])
    o_ref[...] = _layer_norm(alpha * x + mix, l1g_ref[...], l1b_ref[...])


def _mixer_call(l, x, w_in, b_in, caw, cab, lag, lab, wao, bao, cbw, wbo, pw, ps, wmo, l1g, l1b, *, alpha, tb):
    bsz, t_len, d = x.shape
    d_in = w_in.shape[2]
    ka, kb = caw.shape[1], cbw.shape[1]
    n_groups, pg = pw.shape[1], pw.shape[2]
    halo_a = _round_up(ka - 1, SUBLANES)
    halo_b = _round_up(kb - 1, SUBLANES)
    halo_c = _round_up(max(POOL_WINDOWS) - 1, SUBLANES)
    assert t_len % tb == 0 and tb % 128 == 0 and tb >= halo_a
    vec = lambda n: pl.BlockSpec((None, 1, n), lambda i, t: (l, 0, 0))
    kern = functools.partial(_mixer_kernel, alpha=alpha, halo_a=halo_a, halo_b=halo_b, halo_c=halo_c)
    return pl.pallas_call(
        kern,
        out_shape=jax.ShapeDtypeStruct((bsz, t_len, d), F32),
        grid=(bsz, t_len // tb),
        in_specs=[
            pl.BlockSpec((None, tb, d), lambda i, t: (i, t, 0)),
            _resident((None, d, d_in), lambda i, t: (l, 0, 0)),
            vec(d_in),
            pl.BlockSpec((None, ka, d), lambda i, t: (l, 0, 0)),
            vec(d), vec(d), vec(d),
            _resident((None, d, d), lambda i, t: (l, 0, 0)),
            vec(d),
            pl.BlockSpec((None, kb, d), lambda i, t: (l, 0, 0)),
            _resident((None, d, d), lambda i, t: (l, 0, 0)),
            _resident((None, n_groups, pg, pg), lambda i, t: (l, 0, 0, 0)),
            vec(d),
            _resident((None, d, d), lambda i, t: (l, 0, 0)),
            vec(d), vec(d),
        ],
        out_specs=pl.BlockSpec((None, tb, d), lambda i, t: (i, t, 0)),
        scratch_shapes=[
            pltpu.VMEM((halo_a + tb, d), F32),
            pltpu.VMEM((halo_b + tb, d), F32),
            pltpu.VMEM((halo_c + tb, d), F32),
            pltpu.VMEM((tb, d), F32),
            pltpu.VMEM((SUBLANES - 1, halo_a + tb - SUBLANES, d), F32),
        ],
        compiler_params=pltpu.CompilerParams(
            dimension_semantics=("arbitrary", "arbitrary"), vmem_limit_bytes=VMEM_LIMIT_BYTES),
        name="mixer",
    )(x, w_in, b_in, caw, cab, lag, lab, wao, bao, cbw, wbo, pw, ps, wmo, l1g, l1b)


def _xattn_kernel(x_ref, k_ref, v_ref, wq_ref, wo_ref, l2g_ref, l2b_ref, rwt_ref, rb_ref,
                  o_ref, gt_ref, ld_ref, cp_ref, *, alpha):
    tb, d = x_ref.shape
    n_exp = rwt_ref.shape[0]
    dh = d // N_XHEADS
    scale = dh ** -0.5

    x = x_ref[...]
    q = _dot(x.astype(BF16), wq_ref[...])
    heads = []
    for h in range(N_XHEADS):
        qh = q[:, h * dh:(h + 1) * dh].astype(BF16)
        kh = k_ref[:, h * dh:(h + 1) * dh]
        vh = v_ref[:, h * dh:(h + 1) * dh]
        s = lax.dot_general(qh, kh, (((1,), (1,)), ((), ())), preferred_element_type=F32) * scale
        p = jnp.exp(s - jnp.max(s, axis=-1, keepdims=True))
        denom = jnp.sum(p, axis=-1, keepdims=True)
        heads.append(_dot(p.astype(BF16), vh) / denom)
    att = _dot(jnp.concatenate(heads, axis=-1).astype(BF16), wo_ref[...])
    x2 = _layer_norm(alpha * x + att, l2g_ref[...], l2b_ref[...])
    o_ref[...] = x2

    logits = lax.dot_general(rwt_ref[...], x2.astype(BF16), (((1,), (1,)), ((), ())),
                             preferred_element_type=F32) + rb_ref[...]
    eidx = lax.broadcasted_iota(jnp.int32, (n_exp, tb), 0).astype(F32)
    work = logits
    tops, args = [], []
    member = jnp.zeros((n_exp, tb), F32)
    for _ in range(TOP_K):
        mx = jnp.max(work, axis=0, keepdims=True)
        am = jnp.min(jnp.where(work == mx, eidx, float(n_exp)), axis=0, keepdims=True)
        hit = eidx == am
        tops.append(mx)
        args.append(am)
        member = jnp.where(hit, 1.0, member)
        work = jnp.where(hit, -jnp.inf, work)
    exps = [jnp.exp(tv - tops[0]) for tv in tops]
    esum = exps[0]
    for ev in exps[1:]:
        esum = esum + ev

    before = lax.broadcasted_iota(jnp.int32, (tb, tb), 0) < lax.broadcasted_iota(jnp.int32, (tb, tb), 1)
    prefix = _dot(member.astype(BF16), jnp.where(before, 1.0, 0.0).astype(BF16))
    count = jnp.sum(member, axis=1, keepdims=True)
    padded = jnp.floor((count + (SUBLANES - 1)) * (1.0 / SUBLANES)) * SUBLANES
    padded = jnp.broadcast_to(padded, (n_exp, LANES))
    lower = lax.broadcasted_iota(jnp.int32, (n_exp, n_exp), 1) < lax.broadcasted_iota(jnp.int32, (n_exp, n_exp), 0)
    run_start = _dot(jnp.where(lower, 1.0, 0.0).astype(BF16), padded.astype(BF16))
    position = prefix + run_start[:, 0:1]
    gt_ref[...] = jnp.zeros(gt_ref.shape, F32)
    ld_ref[...] = jnp.zeros(ld_ref.shape, jnp.int32)
    for kk in range(TOP_K):
        gt_ref[kk:kk + 1, :] = exps[kk] / esum
        pos = jnp.sum(jnp.where(eidx == args[kk], position, 0.0), axis=0, keepdims=True)
        ld_ref[kk:kk + 1, :] = pos.astype(jnp.int32)
    cp_ref[...] = padded


def _xattn_call(l, x, k_all, v_all, wq, wo, l2g, l2b, rwt, rb, *, alpha, tb):
    bsz, t_len, d = x.shape
    m_len = k_all.shape[2]
    n_exp = rwt.shape[1]
    n_tok = bsz * t_len
    nt = t_len // tb
    assert t_len % tb == 0 and tb % LANES == 0
    vec = lambda n: pl.BlockSpec((None, 1, n), lambda i, t: (l, 0, 0))
    slot = pl.BlockSpec((SUBLANES, tb), lambda i, t: (0, i * nt + t))
    return pl.pallas_call(
        functools.partial(_xattn_kernel, alpha=alpha),
        out_shape=(
            jax.ShapeDtypeStruct((bsz, t_len, d), F32),
            jax.ShapeDtypeStruct((SUBLANES, n_tok), F32),
            jax.ShapeDtypeStruct((SUBLANES, n_tok), jnp.int32),
            jax.ShapeDtypeStruct((n_tok // tb, n_exp, LANES), F32),
        ),
        grid=(bsz, nt),
        in_specs=[
            pl.BlockSpec((None, tb, d), lambda i, t: (i, t, 0)),
            pl.BlockSpec((None, None, m_len, d), lambda i, t: (l, i, 0, 0)),
            pl.BlockSpec((None, None, m_len, d), lambda i, t: (l, i, 0, 0)),
            _resident((None, d, d), lambda i, t: (l, 0, 0)),
            _resident((None, d, d), lambda i, t: (l, 0, 0)),
            vec(d), vec(d),
            pl.BlockSpec((None, n_exp, d), lambda i, t: (l, 0, 0)),
            pl.BlockSpec((None, n_exp, 1), lambda i, t: (l, 0, 0)),
        ],
        out_specs=(
            pl.BlockSpec((None, tb, d), lambda i, t: (i, t, 0)),
            slot, slot,
            pl.BlockSpec((None, n_exp, LANES), lambda i, t: (i * nt + t, 0, 0)),
        ),
        compiler_params=pltpu.CompilerParams(
            dimension_semantics=("arbitrary", "arbitrary"), vmem_limit_bytes=VMEM_LIMIT_BYTES),
        name="xattn_router",
    )(x, k_all, v_all, wq, wo, l2g, l2b, rwt, rb)


def _routing_plan(cp, n_blocks, bm):
    n_exp = cp.shape[1]
    local_start = jnp.cumsum(cp, axis=1) - cp
    tot = jnp.sum(cp, axis=0)
    gend = jnp.cumsum(tot)
    gstart = gend - tot
    global_start = gstart[None, :] + jnp.cumsum(cp, axis=0) - cp
    total = gend[-1]
    tile_chunks = jnp.sum(cp, axis=1) // SUBLANES

    n_items = n_blocks + n_exp + 1
    first_blk = gstart // bm
    nblk = jnp.where(tot > 0, (gend + bm - 1) // bm - first_blk, 0)
    item_end = jnp.cumsum(nblk)
    item_start = item_end - nblk
    it = jnp.arange(n_items, dtype=jnp.int32)
    e_raw = jnp.sum(item_end[None, :] <= it[:, None], axis=1).astype(jnp.int32)
    valid = e_raw < n_exp
    e = jnp.minimum(e_raw, n_exp - 1)
    unused_blk = jnp.minimum((total - 1) // bm + 1 + it - item_end[-1], n_blocks)
    blk = jnp.where(valid, first_blk[e] + it - item_start[e], unused_blk)
    lo = jnp.where(valid, jnp.maximum(gstart[e], blk * bm) - blk * bm, 0)
    hi = jnp.where(valid, jnp.minimum(gend[e], (blk + 1) * bm) - blk * bm, 0)
    first = jnp.logical_or(jnp.logical_not(valid), lo == 0)
    new_expert = jnp.concatenate([jnp.ones((1,), bool), e[1:] != e[:-1]])
    i32 = lambda a: a.astype(jnp.int32)
    runs = tuple(i32(a.reshape(-1)) for a in (cp // SUBLANES, local_start, global_start)) + (i32(tile_chunks),)
    return runs, i32(total.reshape(1)), tuple(map(i32, (blk, e, lo, hi, first, new_expert)))


def _run_copies(nc_ref, ls_ref, gs_ref, step, n_exp, max_chunks, make_copy):
    def per_expert(e, carry):
        idx = step * n_exp + e
        n, local0, global0 = nc_ref[idx], ls_ref[idx], gs_ref[idx]
        for b in range(max_chunks.bit_length()):
            @pl.when(((n >> b) & 1) == 1)
            def _():
                off = ((n >> (b + 1)) << (b + 1)) * SUBLANES
                make_copy(pl.multiple_of(local0 + off, SUBLANES), pl.multiple_of(global0 + off, SUBLANES),
                          SUBLANES << b).start()
        return carry

    lax.fori_loop(0, n_exp, per_expert, 0)


def _wait_chunks(n_chunks, max_chunks, make_copy):
    for b in range(max_chunks.bit_length()):
        @pl.when(((n_chunks >> b) & 1) == 1)
        def _():
            make_copy(0, 0, SUBLANES << b).wait()


def _dispatch_kernel(nc_ref, ls_ref, gs_ref, chunks_ref, total_ref, ld_ref, x_ref, xs_hbm, stage, zeros, sems, zsem,
                     *, n_exp):
    s = pl.program_id(0)
    last = pl.num_programs(0) - 1
    tb, d = x_ref.shape
    n_rows = stage.shape[1]
    slot = s % 2

    run_chunks, tile_chunks = tb // SUBLANES, n_rows // SUBLANES

    def chunk_copy(slot_):
        def make(local_row, global_row, rows):
            return pltpu.make_async_copy(stage.at[slot_, pl.ds(local_row, rows)],
                                         xs_hbm.at[pl.ds(global_row, rows)], sems.at[slot_])
        return make

    rid = lax.broadcasted_iota(jnp.int32, (n_rows, tb), 0)
    onehot = jnp.zeros((n_rows, tb), F32)
    for kk in range(TOP_K):
        onehot = jnp.where(rid == ld_ref[kk:kk + 1, :], 1.0, onehot)
    stage[slot] = _dot(onehot.astype(BF16), x_ref[...].astype(BF16))

    _run_copies(nc_ref, ls_ref, gs_ref, s, n_exp, run_chunks, chunk_copy(slot))

    @pl.when(s > 0)
    def _():
        _wait_chunks(chunks_ref[s - 1], tile_chunks, chunk_copy(1 - slot))

    @pl.when(s == last)
    def _():
        _wait_chunks(chunks_ref[s], tile_chunks, chunk_copy(slot))
        zeros[...] = jnp.zeros(zeros.shape, F32)
        zrows = zeros.shape[0]
        total = total_ref[0]
        rest = xs_hbm.shape[0] - total
        n_big = rest // zrows
        tail0 = total + n_big * zrows

        def big(j):
            return pltpu.make_async_copy(
                zeros, xs_hbm.at[pl.ds(pl.multiple_of(total + j * zrows, SUBLANES), zrows)], zsem)

        def small(j):
            return pltpu.make_async_copy(
                zeros.at[pl.ds(0, SUBLANES)],
                xs_hbm.at[pl.ds(pl.multiple_of(tail0 + j * SUBLANES, SUBLANES), SUBLANES)], zsem)

        for make, count in ((big, n_big), (small, (rest - n_big * zrows) // SUBLANES)):
            lax.fori_loop(0, count, lambda j, c, make=make: (make(j).start(), c)[1], 0)
            lax.fori_loop(0, count, lambda j, c, make=make: (make(j).wait(), c)[1], 0)


def _dispatch_call(runs, total, ldest, x2d, *, n_exp, n_rows_out, stage_rows, tb, bm):
    n_tok, d = x2d.shape
    assert n_tok % tb == 0
    return pl.pallas_call(
        functools.partial(_dispatch_kernel, n_exp=n_exp),
        out_shape=jax.ShapeDtypeStruct((n_rows_out, d), F32),
        grid_spec=pltpu.PrefetchScalarGridSpec(
            num_scalar_prefetch=5,
            grid=(n_tok // tb,),
            in_specs=[pl.BlockSpec((SUBLANES, tb), lambda i, *_: (0, i)),
                      pl.BlockSpec((tb, d), lambda i, *_: (i, 0))],
            out_specs=pl.BlockSpec(memory_space=pl.ANY),
            scratch_shapes=[pltpu.VMEM((2, stage_rows, d), F32), pltpu.VMEM((bm, d), F32),
                            pltpu.SemaphoreType.DMA((2,)), pltpu.SemaphoreType.DMA],
        ),
        compiler_params=pltpu.CompilerParams(
            dimension_semantics=("arbitrary",), vmem_limit_bytes=VMEM_LIMIT_BYTES),
        name="dispatch",
    )(*runs, total, ldest, x2d)


def _expert_kernel(blk_ref, exp_ref, lo_ref, hi_ref, first_ref, new_ref, xs_ref, wgu_ref, bgu_ref, wdn_ref, bdn_ref,
                   ys_ref, wgu_b, wdn_b):
    i = pl.program_id(0)
    bm = xs_ref.shape[0]
    f = wdn_ref.shape[0]
    lo, hi = lo_ref[i], hi_ref[i]

    @pl.when(jnp.logical_and(hi > lo, new_ref[i] == 1))
    def _():
        wgu_b[...] = wgu_ref[...].astype(BF16)
        wdn_b[...] = wdn_ref[...].astype(BF16)

    @pl.when(hi > lo)
    def _():
        gu = _dot(xs_ref[...].astype(BF16), wgu_b[...]) + bgu_ref[...]
        gate = jnp.minimum(gu[:, :f], SWIGLU_LIMIT)
        up = jnp.clip(gu[:, f:], -SWIGLU_LIMIT, SWIGLU_LIMIT)
        act = (up + 1.0) * (gate * jax.nn.sigmoid(gate * SWIGLU_ALPHA))
        y = _dot(act.astype(BF16), wdn_b[...]) + bdn_ref[...]
        row = lax.broadcasted_iota(jnp.int32, (bm, 1), 0)
        y = jnp.where(jnp.logical_and(row >= lo, row < hi), y, 0.0)

        @pl.when(first_ref[i] == 1)
        def _():
            ys_ref[...] = y

        @pl.when(first_ref[i] == 0)
        def _():
            ys_ref[...] = ys_ref[...] + y

    @pl.when(hi <= lo)
    def _():
        ys_ref[...] = jnp.zeros(ys_ref.shape, F32)


def _expert_call(l, items, xs, wgu, bgu, wdn, bdn, *, bm):
    n_rows, d = xs.shape
    f = wdn.shape[2]
    n_items = items[0].shape[0]
    assert n_rows % bm == 0
    return pl.pallas_call(
        _expert_kernel,
        out_shape=jax.ShapeDtypeStruct((n_rows, d), F32),
        grid_spec=pltpu.PrefetchScalarGridSpec(
            num_scalar_prefetch=6,
            grid=(n_items,),
            in_specs=[
                pl.BlockSpec((bm, d), lambda i, blk, ex, *_: (blk[i], 0)),
                pl.BlockSpec((None, None, d, 2 * f), lambda i, blk, ex, *_: (l, ex[i], 0, 0)),
                pl.BlockSpec((None, None, 1, 2 * f), lambda i, blk, ex, *_: (l, ex[i], 0, 0)),
                pl.BlockSpec((None, None, f, d), lambda i, blk, ex, *_: (l, ex[i], 0, 0)),
                pl.BlockSpec((None, None, 1, d), lambda i, blk, ex, *_: (l, ex[i], 0, 0)),
            ],
            out_specs=pl.BlockSpec((bm, d), lambda i, blk, ex, *_: (blk[i], 0)),
            scratch_shapes=[pltpu.VMEM((d, 2 * f), BF16), pltpu.VMEM((f, d), BF16)],
        ),
        compiler_params=pltpu.CompilerParams(
            dimension_semantics=("arbitrary",), vmem_limit_bytes=VMEM_LIMIT_BYTES),
        name="experts",
    )(*items, xs, wgu, bgu, wdn, bdn)


def _combine_kernel(nc_ref, ls_ref, gs_ref, chunks_ref, ld_ref, gt_ref, x_ref, l3g_ref, l3b_ref, ys_hbm, o_ref,
                    stage, sems, *, alpha, n_exp):
    s = pl.program_id(0)
    last = pl.num_programs(0) - 1
    tb, d = x_ref.shape
    n_rows = stage.shape[1]
    slot = s % 2

    run_chunks, tile_chunks = tb // SUBLANES, n_rows // SUBLANES

    def chunk_copy(slot_):
        def make(local_row, global_row, rows):
            return pltpu.make_async_copy(ys_hbm.at[pl.ds(global_row, rows)],
                                         stage.at[slot_, pl.ds(local_row, rows)], sems.at[slot_])
        return make

    @pl.when(s == 0)
    def _():
        stage[...] = jnp.zeros(stage.shape, F32)
        _run_copies(nc_ref, ls_ref, gs_ref, s, n_exp, run_chunks, chunk_copy(slot))

    @pl.when(s < last)
    def _():
        _run_copies(nc_ref, ls_ref, gs_ref, s + 1, n_exp, run_chunks, chunk_copy(1 - slot))

    _wait_chunks(chunks_ref[s], tile_chunks, chunk_copy(slot))

    pos = jnp.transpose(ld_ref[...].astype(F32))
    gates = jnp.transpose(gt_ref[...])
    rid = lax.broadcasted_iota(jnp.int32, (tb, n_rows), 1).astype(F32)
    weights = jnp.zeros((tb, n_rows), F32)
    for kk in range(TOP_K):
        weights = jnp.where(rid == pos[:, kk:kk + 1], gates[:, kk:kk + 1], weights)
    ff = _dot(weights.astype(BF16), stage[slot].astype(BF16))
    o_ref[...] = _layer_norm(alpha * x_ref[...] + ff, l3g_ref[...], l3b_ref[...])


def _combine_call(l, runs, ldest, gates, x2d, l3g, l3b, ys, *, alpha, n_exp, stage_rows, tb):
    n_tok, d = x2d.shape
    assert n_tok % tb == 0 and tb % LANES == 0
    slot = pl.BlockSpec((SUBLANES, tb), lambda i, *_: (0, i))
    vec = pl.BlockSpec((None, 1, d), lambda i, *_: (l, 0, 0))
    return pl.pallas_call(
        functools.partial(_combine_kernel, alpha=alpha, n_exp=n_exp),
        out_shape=jax.ShapeDtypeStruct((n_tok, d), F32),
        grid_spec=pltpu.PrefetchScalarGridSpec(
            num_scalar_prefetch=4,
            grid=(n_tok // tb,),
            in_specs=[slot, slot, pl.BlockSpec((tb, d), lambda i, *_: (i, 0)), vec, vec,
                      pl.BlockSpec(memory_space=pl.ANY)],
            out_specs=pl.BlockSpec((tb, d), lambda i, *_: (i, 0)),
            scratch_shapes=[pltpu.VMEM((2, stage_rows, d), F32), pltpu.SemaphoreType.DMA((2,))],
        ),
        compiler_params=pltpu.CompilerParams(
            dimension_semantics=("arbitrary",), vmem_limit_bytes=VMEM_LIMIT_BYTES),
        name="combine",
    )(*runs, ldest, gates, x2d, l3g, l3b, ys)


def kernel(x, mem, mem_ln_g, mem_ln_b, w_in, b_in, conv_a_w, conv_a_b, ln_a_g, ln_a_b, w_a_out, b_a_out,
           conv_b_w, w_b_out, pool_w, pool_scale, w_mix_out, ln1_g, ln1_b, w_xq, w_xk, w_xv, w_xo, ln2_g, ln2_b,
           router_w, router_b, w_gu, b_gu, w_down, b_down, ln3_g, ln3_b):
    bsz, t_len, d = x.shape
    depth = w_in.shape[0]
    n_exp = router_w.shape[2]
    n_tok = bsz * t_len
    alpha = float((2 * depth) ** 0.25)
    tile = min(256, t_len)
    rtile = min(512, t_len)
    bm = min(512, n_tok * TOP_K)
    n_tiles = n_tok // rtile
    stage_rows = _round_up(TOP_K * rtile + n_exp * (SUBLANES - 1), 2 * LANES)
    n_blocks = pl.cdiv(TOP_K * n_tok + n_tiles * n_exp * (SUBLANES - 1), bm)
    n_rows_out = (n_blocks + 1) * bm

    row = lambda a: a[:, None, :]
    bf = lambda a: a.astype(BF16)
    w_in_b, wao_b, wbo_b, pw_b, wmo_b = bf(w_in), bf(w_a_out), bf(w_b_out), bf(pool_w), bf(w_mix_out)
    wq_b, wo_b = bf(w_xq), bf(w_xo)
    rwt_b = bf(jnp.swapaxes(router_w, 1, 2))
    rb_c = router_b[:, :, None]
    bgu_r, bdn_r = b_gu[:, :, None, :], b_down[:, :, None, :]

    k_all, v_all = _kv_call(mem, mem_ln_g[None, :], mem_ln_b[None, :], bf(w_xk), bf(w_xv))

    for l in range(depth):
        x1 = _mixer_call(l, x, w_in_b, row(b_in), conv_a_w, row(conv_a_b), row(ln_a_g), row(ln_a_b), wao_b,
                         row(b_a_out), conv_b_w, wbo_b, pw_b, row(pool_scale), wmo_b, row(ln1_g), row(ln1_b),
                         alpha=alpha, tb=tile)
        x2, gates, ldest, cp = _xattn_call(l, x1, k_all, v_all, wq_b, wo_b, row(ln2_g), row(ln2_b),
                                           rwt_b, rb_c, alpha=alpha, tb=rtile)
        runs, total, items = _routing_plan(cp[:, :, 0].astype(jnp.int32), n_blocks, bm)
        x2d = x2.reshape(n_tok, d)
        xs = _dispatch_call(runs, total, ldest, x2d, n_exp=n_exp, n_rows_out=n_rows_out, stage_rows=stage_rows,
                            tb=rtile, bm=bm)
        ys = _expert_call(l, items, xs, w_gu, bgu_r, w_down, bdn_r, bm=bm)
        x = _combine_call(l, runs, ldest, gates, x2d, row(ln3_g), row(ln3_b), ys, alpha=alpha, n_exp=n_exp,
                          stage_rows=stage_rows, tb=rtile).reshape(bsz, t_len, d)
    return x
```

```python
import functools

import jax
import jax.numpy as jnp
from jax import lax
from jax.experimental import pallas as pl
from jax.experimental.pallas import tpu as pltpu

N_XHEADS = 4
TOP_K = 4
POOL_WINDOWS = (2, 4, 8, 16)
SWIGLU_LIMIT = 7.0
SWIGLU_ALPHA = 1.702
LN_EPS = 1e-5

SUBLANES = 8
LANES = 128
VMEM_LIMIT_BYTES = 56 * 1024 * 1024

BF16 = jnp.bfloat16
F32 = jnp.float32


def _round_up(n, m):
    return (n + m - 1) // m * m


def _layer_norm(v, g, b):
    mu = jnp.mean(v, axis=-1, keepdims=True)
    c = v - mu
    var = jnp.mean(c * c, axis=-1, keepdims=True)
    return c * lax.rsqrt(var + LN_EPS) * g + b


def _dot(a, b):
    return jnp.dot(a, b, preferred_element_type=F32)


def _resident(block_shape, index_map):
    return pl.BlockSpec(block_shape, index_map, pipeline_mode=pl.Buffered(1))


def _kv_kernel(mem_ref, g_ref, b_ref, wk_ref, wv_ref, k_ref, v_ref):
    mn = _layer_norm(mem_ref[...], g_ref[...], b_ref[...]).astype(BF16)
    k_ref[...] = _dot(mn, wk_ref[...]).astype(BF16)
    v_ref[...] = _dot(mn, wv_ref[...]).astype(BF16)


def _kv_call(mem, g, b, wk, wv):
    bsz, m_len, d = mem.shape
    depth = wk.shape[0]
    out = jax.ShapeDtypeStruct((depth, bsz, m_len, d), BF16)
    return pl.pallas_call(
        _kv_kernel,
        out_shape=(out, out),
        grid=(depth, bsz),
        in_specs=[
            pl.BlockSpec((None, m_len, d), lambda l, i: (i, 0, 0)),
            pl.BlockSpec((1, d), lambda l, i: (0, 0)),
            pl.BlockSpec((1, d), lambda l, i: (0, 0)),
            pl.BlockSpec((None, d, d), lambda l, i: (l, 0, 0)),
            pl.BlockSpec((None, d, d), lambda l, i: (l, 0, 0)),
        ],
        out_specs=(
            pl.BlockSpec((None, None, m_len, d), lambda l, i: (l, i, 0, 0)),
            pl.BlockSpec((None, None, m_len, d), lambda l, i: (l, i, 0, 0)),
        ),
        compiler_params=pltpu.CompilerParams(
            dimension_semantics=("arbitrary", "arbitrary"), vmem_limit_bytes=VMEM_LIMIT_BYTES),
        name="mem_kv",
    )(mem, g, b, wk, wv)


def _causal_taps(buf, shifted, w_ref, halo, tb, row_chunk, out_ref, between=()):
    k_width = w_ref.shape[0]
    d = buf.shape[1]
    first = halo - (k_width - 1)
    n_rows = halo + tb - SUBLANES
    residues = sorted({o % SUBLANES for o in range(first, halo + 1)} - {0})
    for res in residues:
        shifted[res - 1, 0:n_rows, :] = buf[res:res + n_rows, :]
    between = list(between)
    for c0 in range(0, d, LANES):
        cols = slice(c0, c0 + LANES)
        if c0 > 0 and between:
            between.pop(0)()
        for r0 in range(0, tb, row_chunk):
            acc = None
            for o in range(first, halo + 1):
                res = o % SUBLANES
                start = r0 + o - res
                rows = buf[start:start + row_chunk, cols] if res == 0 else shifted[res - 1, start:start + row_chunk, cols]
                term = w_ref[o - first:o - first + 1, cols] * rows
                acc = term if acc is None else acc + term
            out_ref[r0:r0 + row_chunk, cols] = acc
    for job in between:
        job()


def _mixer_kernel(x_ref, w_in_ref, b_in_ref, caw_ref, cab_ref, lag_ref, lab_ref, wao_ref, bao_ref,
                  cbw_ref, wbo_ref, pw_ref, ps_ref, wmo_ref, l1g_ref, l1b_ref, o_ref,
                  abuf, bbuf, cbuf, tmp, shifted, *, alpha, halo_a, halo_b, halo_c):
    tb, d = x_ref.shape
    t = pl.program_id(1)

    @pl.when(t == 0)
    def _():
        abuf[0:halo_a, :] = jnp.zeros((halo_a, d), F32)
        bbuf[0:halo_b, :] = jnp.zeros((halo_b, d), F32)
        cbuf[0:halo_c, :] = jnp.zeros((halo_c, d), F32)

    x = x_ref[...]
    xb = x.astype(BF16)

    def proj(j):
        return _dot(xb, w_in_ref[:, j * d:(j + 1) * d]) + b_in_ref[:, j * d:(j + 1) * d]

    held = {}

    def stage_b():
        bbuf[halo_b:halo_b + tb, :] = proj(3) * proj(4)

    def stage_c():
        cbuf[halo_c:halo_c + tb, :] = proj(5)

    def hold(name, j, fn):
        def job():
            held[name] = fn(proj(j))
        return job

    jobs = [stage_b, hold("gate_b", 2, lambda v: v), stage_c, hold("mix_a", 6, jax.nn.sigmoid),
            hold("mix_b", 7, jax.nn.sigmoid), hold("mix_c", 8, jax.nn.sigmoid)]

    abuf[halo_a:halo_a + tb, :] = proj(0) * jax.nn.sigmoid(proj(1))
    _causal_taps(abuf, shifted, caw_ref, halo_a, tb, 64, tmp, between=jobs)
    a = _layer_norm(tmp[...] + cab_ref[...], lag_ref[...], lab_ref[...])
    a = a * jax.nn.sigmoid(a)
    y_a = _dot(a.astype(BF16), wao_ref[...]) + bao_ref[...]
    m = held["mix_a"] * y_a
    abuf[0:halo_a, :] = abuf[tb:tb + halo_a, :]

    kb = cbw_ref.shape[0]
    u_b = None
    for k in range(kb):
        term = cbw_ref[k:k + 1, :] * bbuf[halo_b - (kb - 1) + k:halo_b - (kb - 1) + k + tb, :]
        u_b = term if u_b is None else u_b + term
    y_b = _dot((held["gate_b"] * u_b).astype(BF16), wbo_ref[...])
    m = m + held["mix_b"] * y_b
    bbuf[0:halo_b, :] = bbuf[tb:tb + halo_b, :]

    n_groups = len(POOL_WINDOWS)
    pg = d // n_groups
    pos = t * tb + lax.broadcasted_iota(jnp.int32, (tb, 1), 0)
    y_parts = []
    for gi, win in enumerate(POOL_WINDOWS):
        lo = gi * pg
        u = cbuf[halo_c:halo_c + tb, lo:lo + pg]
        s = u
        for j in range(1, win):
            s = s + cbuf[halo_c - j:halo_c - j + tb, lo:lo + pg]
        cnt = jnp.minimum(pos + 1, win).astype(F32)
        pooled = s / cnt - u
        y_parts.append(_dot(pooled.astype(BF16), pw_ref[gi]))
    y_c = jnp.concatenate(y_parts, axis=-1) * ps_ref[...]
    m = m + held["mix_c"] * y_c
    cbuf[0:halo_c, :] = cbuf[tb:tb + halo_c, :]

    mix = _dot(m.astype(BF16), wmo_ref[...])
    o_ref[...] = _layer_norm(alpha * x + mix, l1g_ref[...], l1b_ref[...])


def _mixer_call(l, x, w_in, b_in, caw, cab, lag, lab, wao, bao, cbw, wbo, pw, ps, wmo, l1g, l1b, *, alpha, tb):
    bsz, t_len, d = x.shape
    d_in = w_in.shape[2]
    ka, kb = caw.shape[1], cbw.shape[1]
    n_groups, pg = pw.shape[1], pw.shape[2]
    halo_a = _round_up(ka - 1, SUBLANES)
    halo_b = _round_up(kb - 1, SUBLANES)
    halo_c = _round_up(max(POOL_WINDOWS) - 1, SUBLANES)
    assert t_len % tb == 0 and tb % 128 == 0 and tb >= halo_a
    vec = lambda n: pl.BlockSpec((None, 1, n), lambda i, t: (l, 0, 0))
    kern = functools.partial(_mixer_kernel, alpha=alpha, halo_a=halo_a, halo_b=halo_b, halo_c=halo_c)
    return pl.pallas_call(
        kern,
        out_shape=jax.ShapeDtypeStruct((bsz, t_len, d), F32),
        grid=(bsz, t_len // tb),
        in_specs=[
            pl.BlockSpec((None, tb, d), lambda i, t: (i, t, 0)),
            _resident((None, d, d_in), lambda i, t: (l, 0, 0)),
            vec(d_in),
            pl.BlockSpec((None, ka, d), lambda i, t: (l, 0, 0)),
            vec(d), vec(d), vec(d),
            _resident((None, d, d), lambda i, t: (l, 0, 0)),
            vec(d),
            pl.BlockSpec((None, kb, d), lambda i, t: (l, 0, 0)),
            _resident((None, d, d), lambda i, t: (l, 0, 0)),
            _resident((None, n_groups, pg, pg), lambda i, t: (l, 0, 0, 0)),
            vec(d),
            _resident((None, d, d), lambda i, t: (l, 0, 0)),
            vec(d), vec(d),
        ],
        out_specs=pl.BlockSpec((None, tb, d), lambda i, t: (i, t, 0)),
        scratch_shapes=[
            pltpu.VMEM((halo_a + tb, d), F32),
            pltpu.VMEM((halo_b + tb, d), F32),
            pltpu.VMEM((halo_c + tb, d), F32),
            pltpu.VMEM((tb, d), F32),
            pltpu.VMEM((SUBLANES - 1, halo_a + tb - SUBLANES, d), F32),
        ],
        compiler_params=pltpu.CompilerParams(
            dimension_semantics=("arbitrary", "arbitrary"), vmem_limit_bytes=VMEM_LIMIT_BYTES),
        name="mixer",
    )(x, w_in, b_in, caw, cab, lag, lab, wao, bao, cbw, wbo, pw, ps, wmo, l1g, l1b)


def _xattn_kernel(x_ref, k_ref, v_ref, wq_ref, wo_ref, l2g_ref, l2b_ref, rwt_ref, rb_ref,
                  o_ref, gt_ref, ld_ref, cp_ref, *, alpha):
    tb, d = x_ref.shape
    n_exp = rwt_ref.shape[0]
    dh = d // N_XHEADS
    scale = dh ** -0.5

    x = x_ref[...]
    q = _dot(x.astype(BF16), wq_ref[...])
    heads = []
    for h in range(N_XHEADS):
        qh = q[:, h * dh:(h + 1) * dh].astype(BF16)
        kh = k_ref[:, h * dh:(h + 1) * dh]
        vh = v_ref[:, h * dh:(h + 1) * dh]
        s = lax.dot_general(qh, kh, (((1,), (1,)), ((), ())), preferred_element_type=F32) * scale
        p = jnp.exp(s - jnp.max(s, axis=-1, keepdims=True))
        denom = jnp.sum(p, axis=-1, keepdims=True)
        heads.append(_dot(p.astype(BF16), vh) / denom)
    att = _dot(jnp.concatenate(heads, axis=-1).astype(BF16), wo_ref[...])
    x2 = _layer_norm(alpha * x + att, l2g_ref[...], l2b_ref[...])
    o_ref[...] = x2

    logits = lax.dot_general(rwt_ref[...], x2.astype(BF16), (((1,), (1,)), ((), ())),
                             preferred_element_type=F32) + rb_ref[...]
    eidx = lax.broadcasted_iota(jnp.int32, (n_exp, tb), 0).astype(F32)
    work = logits
    tops, args = [], []
    member = jnp.zeros((n_exp, tb), F32)
    for _ in range(TOP_K):
        mx = jnp.max(work, axis=0, keepdims=True)
        am = jnp.min(jnp.where(work == mx, eidx, float(n_exp)), axis=0, keepdims=True)
        hit = eidx == am
        tops.append(mx)
        args.append(am)
        member = jnp.where(hit, 1.0, member)
        work = jnp.where(hit, -jnp.inf, work)
    exps = [jnp.exp(tv - tops[0]) for tv in tops]
    esum = exps[0]
    for ev in exps[1:]:
        esum = esum + ev

    before = lax.broadcasted_iota(jnp.int32, (tb, tb), 0) < lax.broadcasted_iota(jnp.int32, (tb, tb), 1)
    prefix = _dot(member.astype(BF16), jnp.where(before, 1.0, 0.0).astype(BF16))
    count = jnp.sum(member, axis=1, keepdims=True)
    padded = jnp.floor((count + (SUBLANES - 1)) * (1.0 / SUBLANES)) * SUBLANES
    padded = jnp.broadcast_to(padded, (n_exp, LANES))
    lower = lax.broadcasted_iota(jnp.int32, (n_exp, n_exp), 1) < lax.broadcasted_iota(jnp.int32, (n_exp, n_exp), 0)
    run_start = _dot(jnp.where(lower, 1.0, 0.0).astype(BF16), padded.astype(BF16))
    position = prefix + run_start[:, 0:1]
    gt_ref[...] = jnp.zeros(gt_ref.shape, F32)
    ld_ref[...] = jnp.zeros(ld_ref.shape, jnp.int32)
    for kk in range(TOP_K):
        gt_ref[kk:kk + 1, :] = exps[kk] / esum
        pos = jnp.sum(jnp.where(eidx == args[kk], position, 0.0), axis=0, keepdims=True)
        ld_ref[kk:kk + 1, :] = pos.astype(jnp.int32)
    cp_ref[...] = padded


def _xattn_call(l, x, k_all, v_all, wq, wo, l2g, l2b, rwt, rb, *, alpha, tb):
    bsz, t_len, d = x.shape
    m_len = k_all.shape[2]
    n_exp = rwt.shape[1]
    n_tok = bsz * t_len
    nt = t_len // tb
    assert t_len % tb == 0 and tb % LANES == 0
    vec = lambda n: pl.BlockSpec((None, 1, n), lambda i, t: (l, 0, 0))
    slot = pl.BlockSpec((SUBLANES, tb), lambda i, t: (0, i * nt + t))
    return pl.pallas_call(
        functools.partial(_xattn_kernel, alpha=alpha),
        out_shape=(
            jax.ShapeDtypeStruct((bsz, t_len, d), F32),
            jax.ShapeDtypeStruct((SUBLANES, n_tok), F32),
            jax.ShapeDtypeStruct((SUBLANES, n_tok), jnp.int32),
            jax.ShapeDtypeStruct((n_tok // tb, n_exp, LANES), F32),
        ),
        grid=(bsz, nt),
        in_specs=[
            pl.BlockSpec((None, tb, d), lambda i, t: (i, t, 0)),
            pl.BlockSpec((None, None, m_len, d), lambda i, t: (l, i, 0, 0)),
            pl.BlockSpec((None, None, m_len, d), lambda i, t: (l, i, 0, 0)),
            _resident((None, d, d), lambda i, t: (l, 0, 0)),
            _resident((None, d, d), lambda i, t: (l, 0, 0)),
            vec(d), vec(d),
            pl.BlockSpec((None, n_exp, d), lambda i, t: (l, 0, 0)),
            pl.BlockSpec((None, n_exp, 1), lambda i, t: (l, 0, 0)),
        ],
        out_specs=(
            pl.BlockSpec((None, tb, d), lambda i, t: (i, t, 0)),
            slot, slot,
            pl.BlockSpec((None, n_exp, LANES), lambda i, t: (i * nt + t, 0, 0)),
        ),
        compiler_params=pltpu.CompilerParams(
            dimension_semantics=("arbitrary", "arbitrary"), vmem_limit_bytes=VMEM_LIMIT_BYTES),
        name="xattn_router",
    )(x, k_all, v_all, wq, wo, l2g, l2b, rwt, rb)


def _routing_plan(cp, n_blocks, bm):
    n_exp = cp.shape[1]
    local_start = jnp.cumsum(cp, axis=1) - cp
    tot = jnp.sum(cp, axis=0)
    gend = jnp.cumsum(tot)
    gstart = gend - tot
    global_start = gstart[None, :] + jnp.cumsum(cp, axis=0) - cp
    total = gend[-1]
    tile_chunks = jnp.sum(cp, axis=1) // SUBLANES

    n_items = n_blocks + n_exp + 1
    first_blk = gstart // bm
    nblk = jnp.where(tot > 0, (gend + bm - 1) // bm - first_blk, 0)
    item_end = jnp.cumsum(nblk)
    item_start = item_end - nblk
    it = jnp.arange(n_items, dtype=jnp.int32)
    e_raw = jnp.sum(item_end[None, :] <= it[:, None], axis=1).astype(jnp.int32)
    valid = e_raw < n_exp
    e = jnp.minimum(e_raw, n_exp - 1)
    unused_blk = jnp.minimum((total - 1) // bm + 1 + it - item_end[-1], n_blocks)
    blk = jnp.where(valid, first_blk[e] + it - item_start[e], unused_blk)
    lo = jnp.where(valid, jnp.maximum(gstart[e], blk * bm) - blk * bm, 0)
    hi = jnp.where(valid, jnp.minimum(gend[e], (blk + 1) * bm) - blk * bm, 0)
    first = jnp.logical_or(jnp.logical_not(valid), lo == 0)
    new_expert = jnp.concatenate([jnp.ones((1,), bool), e[1:] != e[:-1]])
    i32 = lambda a: a.astype(jnp.int32)
    runs = tuple(i32(a.reshape(-1)) for a in (cp // SUBLANES, local_start, global_start)) + (i32(tile_chunks),)
    return runs, i32(total.reshape(1)), tuple(map(i32, (blk, e, lo, hi, first, new_expert)))


def _run_copies(nc_ref, ls_ref, gs_ref, step, n_exp, max_chunks, make_copy):
    def per_expert(e, carry):
        idx = step * n_exp + e
        n, local0, global0 = nc_ref[idx], ls_ref[idx], gs_ref[idx]
        for b in range(max_chunks.bit_length()):
            @pl.when(((n >> b) & 1) == 1)
            def _():
                off = ((n >> (b + 1)) << (b + 1)) * SUBLANES
                make_copy(pl.multiple_of(local0 + off, SUBLANES), pl.multiple_of(global0 + off, SUBLANES),
                          SUBLANES << b).start()
        return carry

    lax.fori_loop(0, n_exp, per_expert, 0)


def _wait_chunks(n_chunks, max_chunks, make_copy):
    for b in range(max_chunks.bit_length()):
        @pl.when(((n_chunks >> b) & 1) == 1)
        def _():
            make_copy(0, 0, SUBLANES << b).wait()


def _dispatch_kernel(nc_ref, ls_ref, gs_ref, chunks_ref, total_ref, ld_ref, x_ref, xs_hbm, stage, zeros, sems, zsem,
                     *, n_exp):
    s = pl.program_id(0)
    last = pl.num_programs(0) - 1
    tb, d = x_ref.shape
    n_rows = stage.shape[1]
    slot = s % 2

    run_chunks, tile_chunks = tb // SUBLANES, n_rows // SUBLANES

    def chunk_copy(slot_):
        def make(local_row, global_row, rows):
            return pltpu.make_async_copy(stage.at[slot_, pl.ds(local_row, rows)],
                                         xs_hbm.at[pl.ds(global_row, rows)], sems.at[slot_])
        return make

    rid = lax.broadcasted_iota(jnp.int32, (n_rows, tb), 0)
    onehot = jnp.zeros((n_rows, tb), F32)
    for kk in range(TOP_K):
        onehot = jnp.where(rid == ld_ref[kk:kk + 1, :], 1.0, onehot)
    stage[slot] = _dot(onehot.astype(BF16), x_ref[...].astype(BF16))

    _run_copies(nc_ref, ls_ref, gs_ref, s, n_exp, run_chunks, chunk_copy(slot))

    @pl.when(s > 0)
    def _():
        _wait_chunks(chunks_ref[s - 1], tile_chunks, chunk_copy(1 - slot))

    @pl.when(s == last)
    def _():
        _wait_chunks(chunks_ref[s], tile_chunks, chunk_copy(slot))
        zeros[...] = jnp.zeros(zeros.shape, F32)
        zrows = zeros.shape[0]
        total = total_ref[0]
        rest = xs_hbm.shape[0] - total
        n_big = rest // zrows
        tail0 = total + n_big * zrows

        def big(j):
            return pltpu.make_async_copy(
                zeros, xs_hbm.at[pl.ds(pl.multiple_of(total + j * zrows, SUBLANES), zrows)], zsem)

        def small(j):
            return pltpu.make_async_copy(
                zeros.at[pl.ds(0, SUBLANES)],
                xs_hbm.at[pl.ds(pl.multiple_of(tail0 + j * SUBLANES, SUBLANES), SUBLANES)], zsem)

        for make, count in ((big, n_big), (small, (rest - n_big * zrows) // SUBLANES)):
            lax.fori_loop(0, count, lambda j, c, make=make: (make(j).start(), c)[1], 0)
            lax.fori_loop(0, count, lambda j, c, make=make: (make(j).wait(), c)[1], 0)


def _dispatch_call(runs, total, ldest, x2d, *, n_exp, n_rows_out, stage_rows, tb, bm):
    n_tok, d = x2d.shape
    assert n_tok % tb == 0
    return pl.pallas_call(
        functools.partial(_dispatch_kernel, n_exp=n_exp),
        out_shape=jax.ShapeDtypeStruct((n_rows_out, d), F32),
        grid_spec=pltpu.PrefetchScalarGridSpec(
            num_scalar_prefetch=5,
            grid=(n_tok // tb,),
            in_specs=[pl.BlockSpec((SUBLANES, tb), lambda i, *_: (0, i)),
                      pl.BlockSpec((tb, d), lambda i, *_: (i, 0))],
            out_specs=pl.BlockSpec(memory_space=pl.ANY),
            scratch_shapes=[pltpu.VMEM((2, stage_rows, d), F32), pltpu.VMEM((bm, d), F32),
                            pltpu.SemaphoreType.DMA((2,)), pltpu.SemaphoreType.DMA],
        ),
        compiler_params=pltpu.CompilerParams(
            dimension_semantics=("arbitrary",), vmem_limit_bytes=VMEM_LIMIT_BYTES),
        name="dispatch",
    )(*runs, total, ldest, x2d)


def _expert_kernel(blk_ref, exp_ref, lo_ref, hi_ref, first_ref, new_ref, xs_ref, wgu_ref, bgu_ref, wdn_ref, bdn_ref,
                   ys_ref, wgu_b, wdn_b):
    i = pl.program_id(0)
    bm = xs_ref.shape[0]
    f = wdn_ref.shape[0]
    lo, hi = lo_ref[i], hi_ref[i]

    @pl.when(jnp.logical_and(hi > lo, new_ref[i] == 1))
    def _():
        wgu_b[...] = wgu_ref[...].astype(BF16)
        wdn_b[...] = wdn_ref[...].astype(BF16)

    @pl.when(hi > lo)
    def _():
        gu = _dot(xs_ref[...].astype(BF16), wgu_b[...]) + bgu_ref[...]
        gate = jnp.minimum(gu[:, :f], SWIGLU_LIMIT)
        up = jnp.clip(gu[:, f:], -SWIGLU_LIMIT, SWIGLU_LIMIT)
        act = (up + 1.0) * (gate * jax.nn.sigmoid(gate * SWIGLU_ALPHA))
        y = _dot(act.astype(BF16), wdn_b[...]) + bdn_ref[...]
        row = lax.broadcasted_iota(jnp.int32, (bm, 1), 0)
        y = jnp.where(jnp.logical_and(row >= lo, row < hi), y, 0.0)

        @pl.when(first_ref[i] == 1)
        def _():
            ys_ref[...] = y

        @pl.when(first_ref[i] == 0)
        def _():
            ys_ref[...] = ys_ref[...] + y

    @pl.when(hi <= lo)
    def _():
        ys_ref[...] = jnp.zeros(ys_ref.shape, F32)


def _expert_call(l, items, xs, wgu, bgu, wdn, bdn, *, bm):
    n_rows, d = xs.shape
    f = wdn.shape[2]
    n_items = items[0].shape[0]
    assert n_rows % bm == 0
    return pl.pallas_call(
        _expert_kernel,
        out_shape=jax.ShapeDtypeStruct((n_rows, d), F32),
        grid_spec=pltpu.PrefetchScalarGridSpec(
            num_scalar_prefetch=6,
            grid=(n_items,),
            in_specs=[
                pl.BlockSpec((bm, d), lambda i, blk, ex, *_: (blk[i], 0)),
                pl.BlockSpec((None, None, d, 2 * f), lambda i, blk, ex, *_: (l, ex[i], 0, 0)),
                pl.BlockSpec((None, None, 1, 2 * f), lambda i, blk, ex, *_: (l, ex[i], 0, 0)),
                pl.BlockSpec((None, None, f, d), lambda i, blk, ex, *_: (l, ex[i], 0, 0)),
                pl.BlockSpec((None, None, 1, d), lambda i, blk, ex, *_: (l, ex[i], 0, 0)),
            ],
            out_specs=pl.BlockSpec((bm, d), lambda i, blk, ex, *_: (blk[i], 0)),
            scratch_shapes=[pltpu.VMEM((d, 2 * f), BF16), pltpu.VMEM((f, d), BF16)],
        ),
        compiler_params=pltpu.CompilerParams(
            dimension_semantics=("arbitrary",), vmem_limit_bytes=VMEM_LIMIT_BYTES),
        name="experts",
    )(*items, xs, wgu, bgu, wdn, bdn)


def _combine_kernel(nc_ref, ls_ref, gs_ref, chunks_ref, ld_ref, gt_ref, x_ref, l3g_ref, l3b_ref, ys_hbm, o_ref,
                    stage, sems, *, alpha, n_exp):
    s = pl.program_id(0)
    last = pl.num_programs(0) - 1
    tb, d = x_ref.shape
    n_rows = stage.shape[1]
    slot = s % 2

    run_chunks, tile_chunks = tb // SUBLANES, n_rows // SUBLANES

    def chunk_copy(slot_):
        def make(local_row, global_row, rows):
            return pltpu.make_async_copy(ys_hbm.at[pl.ds(global_row, rows)],
                                         stage.at[slot_, pl.ds(local_row, rows)], sems.at[slot_])
        return make

    @pl.when(s == 0)
    def _():
        stage[...] = jnp.zeros(stage.shape, F32)
        _run_copies(nc_ref, ls_ref, gs_ref, s, n_exp, run_chunks, chunk_copy(slot))

    @pl.when(s < last)
    def _():
        _run_copies(nc_ref, ls_ref, gs_ref, s + 1, n_exp, run_chunks, chunk_copy(1 - slot))

    _wait_chunks(chunks_ref[s], tile_chunks, chunk_copy(slot))

    pos = jnp.transpose(ld_ref[...].astype(F32))
    gates = jnp.transpose(gt_ref[...])
    rid = lax.broadcasted_iota(jnp.int32, (tb, n_rows), 1).astype(F32)
    weights = jnp.zeros((tb, n_rows), F32)
    for kk in range(TOP_K):
        weights = jnp.where(rid == pos[:, kk:kk + 1], gates[:, kk:kk + 1], weights)
    ff = _dot(weights.astype(BF16), stage[slot].astype(BF16))
    o_ref[...] = _layer_norm(alpha * x_ref[...] + ff, l3g_ref[...], l3b_ref[...])


def _combine_call(l, runs, ldest, gates, x2d, l3g, l3b, ys, *, alpha, n_exp, stage_rows, tb):
    n_tok, d = x2d.shape
    assert n_tok % tb == 0 and tb % LANES == 0
    slot = pl.BlockSpec((SUBLANES, tb), lambda i, *_: (0, i))
    vec = pl.BlockSpec((None, 1, d), lambda i, *_: (l, 0, 0))
    return pl.pallas_call(
        functools.partial(_combine_kernel, alpha=alpha, n_exp=n_exp),
        out_shape=jax.ShapeDtypeStruct((n_tok, d), F32),
        grid_spec=pltpu.PrefetchScalarGridSpec(
            num_scalar_prefetch=4,
            grid=(n_tok // tb,),
            in_specs=[slot, slot, pl.BlockSpec((tb, d), lambda i, *_: (i, 0)), vec, vec,
                      pl.BlockSpec(memory_space=pl.ANY)],
            out_specs=pl.BlockSpec((tb, d), lambda i, *_: (i, 0)),
            scratch_shapes=[pltpu.VMEM((2, stage_rows, d), F32), pltpu.SemaphoreType.DMA((2,))],
        ),
        compiler_params=pltpu.CompilerParams(
            dimension_semantics=("arbitrary",), vmem_limit_bytes=VMEM_LIMIT_BYTES),
        name="combine",
    )(*runs, ldest, gates, x2d, l3g, l3b, ys)


def kernel(x, mem, mem_ln_g, mem_ln_b, w_in, b_in, conv_a_w, conv_a_b, ln_a_g, ln_a_b, w_a_out, b_a_out,
           conv_b_w, w_b_out, pool_w, pool_scale, w_mix_out, ln1_g, ln1_b, w_xq, w_xk, w_xv, w_xo, ln2_g, ln2_b,
           router_w, router_b, w_gu, b_gu, w_down, b_down, ln3_g, ln3_b):
    bsz, t_len, d = x.shape
    depth = w_in.shape[0]
    n_exp = router_w.shape[2]
    n_tok = bsz * t_len
    alpha = float((2 * depth) ** 0.25)
    tile = min(512, t_len)
    rtile = min(512, t_len)
    bm = min(512, n_tok * TOP_K)
    n_tiles = n_tok // rtile
    stage_rows = _round_up(TOP_K * rtile + n_exp * (SUBLANES - 1), 2 * LANES)
    n_blocks = pl.cdiv(TOP_K * n_tok + n_tiles * n_exp * (SUBLANES - 1), bm)
    n_rows_out = (n_blocks + 1) * bm

    row = lambda a: a[:, None, :]
    bf = lambda a: a.astype(BF16)
    w_in_b, wao_b, wbo_b, pw_b, wmo_b = bf(w_in), bf(w_a_out), bf(w_b_out), bf(pool_w), bf(w_mix_out)
    wq_b, wo_b = bf(w_xq), bf(w_xo)
    rwt_b = bf(jnp.swapaxes(router_w, 1, 2))
    rb_c = router_b[:, :, None]
    bgu_r, bdn_r = b_gu[:, :, None, :], b_down[:, :, None, :]

    k_all, v_all = _kv_call(mem, mem_ln_g[None, :], mem_ln_b[None, :], bf(w_xk), bf(w_xv))

    for l in range(depth):
        x1 = _mixer_call(l, x, w_in_b, row(b_in), conv_a_w, row(conv_a_b), row(ln_a_g), row(ln_a_b), wao_b,
                         row(b_a_out), conv_b_w, wbo_b, pw_b, row(pool_scale), wmo_b, row(ln1_g), row(ln1_b),
                         alpha=alpha, tb=tile)
        x2, gates, ldest, cp = _xattn_call(l, x1, k_all, v_all, wq_b, wo_b, row(ln2_g), row(ln2_b),
                                           rwt_b, rb_c, alpha=alpha, tb=rtile)
        runs, total, items = _routing_plan(cp[:, :, 0].astype(jnp.int32), n_blocks, bm)
        x2d = x2.reshape(n_tok, d)
        xs = _dispatch_call(runs, total, ldest, x2d, n_exp=n_exp, n_rows_out=n_rows_out, stage_rows=stage_rows,
                            tb=rtile, bm=bm)
        ys = _expert_call(l, items, xs, w_gu, bgu_r, w_down, bdn_r, bm=bm)
        x = _combine_call(l, runs, ldest, gates, x2d, row(ln3_g), row(ln3_b), ys, alpha=alpha, n_exp=n_exp,
                          stage_rows=stage_rows, tb=rtile).reshape(bsz, t_len, d)
    return x
```

```python
import functools

import jax
import jax.numpy as jnp
from jax import lax
from jax.experimental import pallas as pl
from jax.experimental.pallas import tpu as pltpu

N_XHEADS = 4
TOP_K = 4
POOL_WINDOWS = (2, 4, 8, 16)
SWIGLU_LIMIT = 7.0
SWIGLU_ALPHA = 1.702
LN_EPS = 1e-5

SUBLANES = 8
LANES = 128
VMEM_LIMIT_BYTES = 56 * 1024 * 1024
CONV_ROW_CHUNK = 64

BF16 = jnp.bfloat16
F32 = jnp.float32


def _round_up(n, m):
    return (n + m - 1) // m * m


def _layer_norm(v, g, b):
    mu = jnp.mean(v, axis=-1, keepdims=True)
    c = v - mu
    var = jnp.mean(c * c, axis=-1, keepdims=True)
    return c * lax.rsqrt(var + LN_EPS) * g + b


def _dot(a, b):
    return jnp.dot(a, b, preferred_element_type=F32)


def _resident(block_shape, index_map):
    return pl.BlockSpec(block_shape, index_map, pipeline_mode=pl.Buffered(1))


def _kv_kernel(mem_ref, g_ref, b_ref, wk_ref, wv_ref, k_ref, v_ref):
    mn = _layer_norm(mem_ref[...], g_ref[...], b_ref[...]).astype(BF16)
    k_ref[...] = _dot(mn, wk_ref[...]).astype(BF16)
    v_ref[...] = _dot(mn, wv_ref[...]).astype(BF16)


def _kv_call(mem, g, b, wk, wv):
    bsz, m_len, d = mem.shape
    depth = wk.shape[0]
    out = jax.ShapeDtypeStruct((depth, bsz, m_len, d), BF16)
    return pl.pallas_call(
        _kv_kernel,
        out_shape=(out, out),
        grid=(depth, bsz),
        in_specs=[
            pl.BlockSpec((None, m_len, d), lambda l, i: (i, 0, 0)),
            pl.BlockSpec((1, d), lambda l, i: (0, 0)),
            pl.BlockSpec((1, d), lambda l, i: (0, 0)),
            pl.BlockSpec((None, d, d), lambda l, i: (l, 0, 0)),
            pl.BlockSpec((None, d, d), lambda l, i: (l, 0, 0)),
        ],
        out_specs=(
            pl.BlockSpec((None, None, m_len, d), lambda l, i: (l, i, 0, 0)),
            pl.BlockSpec((None, None, m_len, d), lambda l, i: (l, i, 0, 0)),
        ),
        compiler_params=pltpu.CompilerParams(
            dimension_semantics=("arbitrary", "arbitrary"), vmem_limit_bytes=VMEM_LIMIT_BYTES),
        name="mem_kv",
    )(mem, g, b, wk, wv)


def _causal_taps(buf, shifted, w_ref, halo, tb, row_chunk, out_ref, between=()):
    k_width = w_ref.shape[0]
    d = buf.shape[1]
    first = halo - (k_width - 1)
    n_rows = halo + tb - SUBLANES
    residues = sorted({o % SUBLANES for o in range(first, halo + 1)} - {0})
    for res in residues:
        shifted[res - 1, 0:n_rows, :] = buf[res:res + n_rows, :]
    between = list(between)
    for c0 in range(0, d, LANES):
        cols = slice(c0, c0 + LANES)
        if c0 > 0 and between:
            between.pop(0)()
        for r0 in range(0, tb, row_chunk):
            acc = None
            for o in range(first, halo + 1):
                res = o % SUBLANES
                start = r0 + o - res
                rows = buf[start:start + row_chunk, cols] if res == 0 else shifted[res - 1, start:start + row_chunk, cols]
                term = w_ref[o - first:o - first + 1, cols] * rows
                acc = term if acc is None else acc + term
            out_ref[r0:r0 + row_chunk, cols] = acc
    for job in between:
        job()


def _mixer_kernel(x_ref, w_in_ref, b_in_ref, caw_ref, cab_ref, lag_ref, lab_ref, wao_ref, bao_ref,
                  cbw_ref, wbo_ref, pw_ref, ps_ref, wmo_ref, l1g_ref, l1b_ref, o_ref,
                  abuf, bbuf, cbuf, tmp, shifted, *, alpha, halo_a, halo_b, halo_c):
    tb, d = x_ref.shape
    t = pl.program_id(1)

    @pl.when(t == 0)
    def _():
        abuf[0:halo_a, :] = jnp.zeros((halo_a, d), F32)
        bbuf[0:halo_b, :] = jnp.zeros((halo_b, d), F32)
        cbuf[0:halo_c, :] = jnp.zeros((halo_c, d), F32)

    x = x_ref[...]
    xb = x.astype(BF16)

    def proj(j):
        return _dot(xb, w_in_ref[:, j * d:(j + 1) * d]) + b_in_ref[:, j * d:(j + 1) * d]

    held = {}

    def stage_b():
        bbuf[halo_b:halo_b + tb, :] = proj(3) * proj(4)

    def stage_c():
        cbuf[halo_c:halo_c + tb, :] = proj(5)

    def hold(name, j, fn):
        def job():
            held[name] = fn(proj(j))
        return job

    jobs = [stage_b, hold("gate_b", 2, lambda v: v), stage_c, hold("mix_a", 6, jax.nn.sigmoid),
            hold("mix_b", 7, jax.nn.sigmoid), hold("mix_c", 8, jax.nn.sigmoid)]

    abuf[halo_a:halo_a + tb, :] = proj(0) * jax.nn.sigmoid(proj(1))
    _causal_taps(abuf, shifted, caw_ref, halo_a, tb, CONV_ROW_CHUNK, tmp, between=jobs)
    a = _layer_norm(tmp[...] + cab_ref[...], lag_ref[...], lab_ref[...])
    a = a * jax.nn.sigmoid(a)
    y_a = _dot(a.astype(BF16), wao_ref[...]) + bao_ref[...]
    m = held["mix_a"] * y_a
    abuf[0:halo_a, :] = abuf[tb:tb + halo_a, :]

    kb = cbw_ref.shape[0]
    u_b = None
    for k in range(kb):
        term = cbw_ref[k:k + 1, :] * bbuf[halo_b - (kb - 1) + k:halo_b - (kb - 1) + k + tb, :]
        u_b = term if u_b is None else u_b + term
    y_b = _dot((held["gate_b"] * u_b).astype(BF16), wbo_ref[...])
    m = m + held["mix_b"] * y_b
    bbuf[0:halo_b, :] = bbuf[tb:tb + halo_b, :]

    n_groups = len(POOL_WINDOWS)
    pg = d // n_groups
    pos = t * tb + lax.broadcasted_iota(jnp.int32, (tb, 1), 0)
    y_parts = []
    for gi, win in enumerate(POOL_WINDOWS):
        lo = gi * pg
        u = cbuf[halo_c:halo_c + tb, lo:lo + pg]
        s = u
        for j in range(1, win):
            s = s + cbuf[halo_c - j:halo_c - j + tb, lo:lo + pg]
        cnt = jnp.minimum(pos + 1, win).astype(F32)
        pooled = s / cnt - u
        y_parts.append(_dot(pooled.astype(BF16), pw_ref[gi]))
    y_c = jnp.concatenate(y_parts, axis=-1) * ps_ref[...]
    m = m + held["mix_c"] * y_c
    cbuf[0:halo_c, :] = cbuf[tb:tb + halo_c, :]

    mix = _dot(m.astype(BF16), wmo_ref[...])
    o_ref[...] = _layer_norm(alpha * x + mix, l1g_ref[...], l1b_ref[...])


def _mixer_call(l, x, w_in, b_in, caw, cab, lag, lab, wao, bao, cbw, wbo, pw, ps, wmo, l1g, l1b, *, alpha, tb):
    bsz, t_len, d = x.shape
    d_in = w_in.shape[2]
    ka, kb = caw.shape[1], cbw.shape[1]
    n_groups, pg = pw.shape[1], pw.shape[2]
    halo_a = _round_up(ka - 1, SUBLANES)
    halo_b = _round_up(kb - 1, SUBLANES)
    halo_c = _round_up(max(POOL_WINDOWS) - 1, SUBLANES)
    assert t_len % tb == 0 and tb % LANES == 0 and tb >= halo_a
    vec = lambda n: pl.BlockSpec((None, 1, n), lambda i, t: (l, 0, 0))
    kern = functools.partial(_mixer_kernel, alpha=alpha, halo_a=halo_a, halo_b=halo_b, halo_c=halo_c)
    return pl.pallas_call(
        kern,
        out_shape=jax.ShapeDtypeStruct((bsz, t_len, d), F32),
        grid=(bsz, t_len // tb),
        in_specs=[
            pl.BlockSpec((None, tb, d), lambda i, t: (i, t, 0)),
            _resident((None, d, d_in), lambda i, t: (l, 0, 0)),
            vec(d_in),
            pl.BlockSpec((None, ka, d), lambda i, t: (l, 0, 0)),
            vec(d), vec(d), vec(d),
            _resident((None, d, d), lambda i, t: (l, 0, 0)),
            vec(d),
            pl.BlockSpec((None, kb, d), lambda i, t: (l, 0, 0)),
            _resident((None, d, d), lambda i, t: (l, 0, 0)),
            _resident((None, n_groups, pg, pg), lambda i, t: (l, 0, 0, 0)),
            vec(d),
            _resident((None, d, d), lambda i, t: (l, 0, 0)),
            vec(d), vec(d),
        ],
        out_specs=pl.BlockSpec((None, tb, d), lambda i, t: (i, t, 0)),
        scratch_shapes=[
            pltpu.VMEM((halo_a + tb, d), F32),
            pltpu.VMEM((halo_b + tb, d), F32),
            pltpu.VMEM((halo_c + tb, d), F32),
            pltpu.VMEM((tb, d), F32),
            pltpu.VMEM((SUBLANES - 1, halo_a + tb - SUBLANES, d), F32),
        ],
        compiler_params=pltpu.CompilerParams(
            dimension_semantics=("arbitrary", "arbitrary"), vmem_limit_bytes=VMEM_LIMIT_BYTES),
        name="mixer",
    )(x, w_in, b_in, caw, cab, lag, lab, wao, bao, cbw, wbo, pw, ps, wmo, l1g, l1b)


def _xattn_kernel(x_ref, k_ref, v_ref, wq_ref, wo_ref, l2g_ref, l2b_ref, rwt_ref, rb_ref,
                  o_ref, gt_ref, ld_ref, cp_ref, *, alpha):
    tb, d = x_ref.shape
    n_exp = rwt_ref.shape[0]
    dh = d // N_XHEADS
    scale = dh ** -0.5

    x = x_ref[...]
    q = _dot(x.astype(BF16), wq_ref[...])
    heads = []
    for h in range(N_XHEADS):
        qh = q[:, h * dh:(h + 1) * dh].astype(BF16)
        kh = k_ref[:, h * dh:(h + 1) * dh]
        vh = v_ref[:, h * dh:(h + 1) * dh]
        s = lax.dot_general(qh, kh, (((1,), (1,)), ((), ())), preferred_element_type=F32) * scale
        p = jnp.exp(s - jnp.max(s, axis=-1, keepdims=True))
        denom = jnp.sum(p, axis=-1, keepdims=True)
        heads.append(_dot(p.astype(BF16), vh) / denom)
    att = _dot(jnp.concatenate(heads, axis=-1).astype(BF16), wo_ref[...])
    x2 = _layer_norm(alpha * x + att, l2g_ref[...], l2b_ref[...])
    o_ref[...] = x2

    logits = lax.dot_general(rwt_ref[...], x2.astype(BF16), (((1,), (1,)), ((), ())),
                             preferred_element_type=F32) + rb_ref[...]
    eidx = lax.broadcasted_iota(jnp.int32, (n_exp, tb), 0).astype(F32)
    work = logits
    tops, args = [], []
    member = jnp.zeros((n_exp, tb), F32)
    for _ in range(TOP_K):
        mx = jnp.max(work, axis=0, keepdims=True)
        am = jnp.min(jnp.where(work == mx, eidx, float(n_exp)), axis=0, keepdims=True)
        hit = eidx == am
        tops.append(mx)
        args.append(am)
        member = jnp.where(hit, 1.0, member)
        work = jnp.where(hit, -jnp.inf, work)
    exps = [jnp.exp(tv - tops[0]) for tv in tops]
    esum = exps[0]
    for ev in exps[1:]:
        esum = esum + ev

    before = lax.broadcasted_iota(jnp.int32, (tb, tb), 0) < lax.broadcasted_iota(jnp.int32, (tb, tb), 1)
    prefix = _dot(member.astype(BF16), jnp.where(before, 1.0, 0.0).astype(BF16))
    count = jnp.sum(member, axis=1, keepdims=True)
    padded = jnp.floor((count + (SUBLANES - 1)) * (1.0 / SUBLANES)) * SUBLANES
    padded = jnp.broadcast_to(padded, (n_exp, LANES))
    lower = lax.broadcasted_iota(jnp.int32, (n_exp, n_exp), 1) < lax.broadcasted_iota(jnp.int32, (n_exp, n_exp), 0)
    run_start = _dot(jnp.where(lower, 1.0, 0.0).astype(BF16), padded.astype(BF16))
    position = prefix + run_start[:, 0:1]
    gt_ref[...] = jnp.zeros(gt_ref.shape, F32)
    ld_ref[...] = jnp.zeros(ld_ref.shape, jnp.int32)
    for kk in range(TOP_K):
        gt_ref[kk:kk + 1, :] = exps[kk] / esum
        pos = jnp.sum(jnp.where(eidx == args[kk], position, 0.0), axis=0, keepdims=True)
        ld_ref[kk:kk + 1, :] = pos.astype(jnp.int32)
    cp_ref[...] = padded


def _xattn_call(l, x, k_all, v_all, wq, wo, l2g, l2b, rwt, rb, *, alpha, tb):
    bsz, t_len, d = x.shape
    m_len = k_all.shape[2]
    n_exp = rwt.shape[1]
    n_tok = bsz * t_len
    nt = t_len // tb
    assert t_len % tb == 0 and tb % LANES == 0
    vec = lambda n: pl.BlockSpec((None, 1, n), lambda i, t: (l, 0, 0))
    slot = pl.BlockSpec((SUBLANES, tb), lambda i, t: (0, i * nt + t))
    return pl.pallas_call(
        functools.partial(_xattn_kernel, alpha=alpha),
        out_shape=(
            jax.ShapeDtypeStruct((bsz, t_len, d), F32),
            jax.ShapeDtypeStruct((SUBLANES, n_tok), F32),
            jax.ShapeDtypeStruct((SUBLANES, n_tok), jnp.int32),
            jax.ShapeDtypeStruct((n_tok // tb, n_exp, LANES), F32),
        ),
        grid=(bsz, nt),
        in_specs=[
            pl.BlockSpec((None, tb, d), lambda i, t: (i, t, 0)),
            pl.BlockSpec((None, None, m_len, d), lambda i, t: (l, i, 0, 0)),
            pl.BlockSpec((None, None, m_len, d), lambda i, t: (l, i, 0, 0)),
            _resident((None, d, d), lambda i, t: (l, 0, 0)),
            _resident((None, d, d), lambda i, t: (l, 0, 0)),
            vec(d), vec(d),
            pl.BlockSpec((None, n_exp, d), lambda i, t: (l, 0, 0)),
            pl.BlockSpec((None, n_exp, 1), lambda i, t: (l, 0, 0)),
        ],
        out_specs=(
            pl.BlockSpec((None, tb, d), lambda i, t: (i, t, 0)),
            slot, slot,
            pl.BlockSpec((None, n_exp, LANES), lambda i, t: (i * nt + t, 0, 0)),
        ),
        compiler_params=pltpu.CompilerParams(
            dimension_semantics=("arbitrary", "arbitrary"), vmem_limit_bytes=VMEM_LIMIT_BYTES),
        name="xattn_router",
    )(x, k_all, v_all, wq, wo, l2g, l2b, rwt, rb)


def _routing_plan(cp, n_blocks, bm):
    n_exp = cp.shape[1]
    local_start = jnp.cumsum(cp, axis=1) - cp
    tot = jnp.sum(cp, axis=0)
    gend = jnp.cumsum(tot)
    gstart = gend - tot
    global_start = gstart[None, :] + jnp.cumsum(cp, axis=0) - cp
    total = gend[-1]
    tile_chunks = jnp.sum(cp, axis=1) // SUBLANES

    n_items = n_blocks + n_exp + 1
    first_blk = gstart // bm
    nblk = jnp.where(tot > 0, (gend + bm - 1) // bm - first_blk, 0)
    item_end = jnp.cumsum(nblk)
    item_start = item_end - nblk
    it = jnp.arange(n_items, dtype=jnp.int32)
    e_raw = jnp.sum(item_end[None, :] <= it[:, None], axis=1).astype(jnp.int32)
    valid = e_raw < n_exp
    e = jnp.minimum(e_raw, n_exp - 1)
    unused_blk = jnp.minimum((total - 1) // bm + 1 + it - item_end[-1], n_blocks)
    is_e = e[:, None] == jnp.arange(n_exp, dtype=jnp.int32)[None, :]
    of_e = lambda table: jnp.sum(jnp.where(is_e, table[None, :], 0), axis=1)
    blk = jnp.where(valid, of_e(first_blk) + it - of_e(item_start), unused_blk)
    lo = jnp.where(valid, jnp.maximum(of_e(gstart), blk * bm) - blk * bm, 0)
    hi = jnp.where(valid, jnp.minimum(of_e(gend), (blk + 1) * bm) - blk * bm, 0)
    first = jnp.logical_or(jnp.logical_not(valid), lo == 0)
    new_expert = jnp.concatenate([jnp.ones((1,), bool), e[1:] != e[:-1]])
    i32 = lambda a: a.astype(jnp.int32)
    runs = tuple(i32(a.reshape(-1)) for a in (cp // SUBLANES, local_start, global_start)) + (i32(tile_chunks),)
    return runs, i32(total.reshape(1)), tuple(map(i32, (blk, e, lo, hi, first, new_expert)))


def _run_copies(nc_ref, ls_ref, gs_ref, step, n_exp, max_chunks, make_copy):
    def per_expert(e, carry):
        idx = step * n_exp + e
        n, local0, global0 = nc_ref[idx], ls_ref[idx], gs_ref[idx]
        for b in range(max_chunks.bit_length()):
            @pl.when(((n >> b) & 1) == 1)
            def _():
                off = ((n >> (b + 1)) << (b + 1)) * SUBLANES
                make_copy(pl.multiple_of(local0 + off, SUBLANES), pl.multiple_of(global0 + off, SUBLANES),
                          SUBLANES << b).start()
        return carry

    lax.fori_loop(0, n_exp, per_expert, 0)


def _wait_chunks(n_chunks, max_chunks, make_copy):
    for b in range(max_chunks.bit_length()):
        @pl.when(((n_chunks >> b) & 1) == 1)
        def _():
            make_copy(0, 0, SUBLANES << b).wait()


def _dispatch_kernel(nc_ref, ls_ref, gs_ref, chunks_ref, total_ref, ld_ref, x_ref, xs_hbm, stage, zeros, sems, zsem,
                     *, n_exp):
    s = pl.program_id(0)
    last = pl.num_programs(0) - 1
    tb, d = x_ref.shape
    n_rows = stage.shape[1]
    slot = s % 2

    run_chunks, tile_chunks = tb // SUBLANES, n_rows // SUBLANES

    def chunk_copy(slot_):
        def make(local_row, global_row, rows):
            return pltpu.make_async_copy(stage.at[slot_, pl.ds(local_row, rows)],
                                         xs_hbm.at[pl.ds(global_row, rows)], sems.at[slot_])
        return make

    rid = lax.broadcasted_iota(jnp.int32, (n_rows, tb), 0)
    onehot = jnp.zeros((n_rows, tb), F32)
    for kk in range(TOP_K):
        onehot = jnp.where(rid == ld_ref[kk:kk + 1, :], 1.0, onehot)
    stage[slot] = _dot(onehot.astype(BF16), x_ref[...].astype(BF16))

    _run_copies(nc_ref, ls_ref, gs_ref, s, n_exp, run_chunks, chunk_copy(slot))

    @pl.when(s > 0)
    def _():
        _wait_chunks(chunks_ref[s - 1], tile_chunks, chunk_copy(1 - slot))

    @pl.when(s == last)
    def _():
        _wait_chunks(chunks_ref[s], tile_chunks, chunk_copy(slot))
        zeros[...] = jnp.zeros(zeros.shape, F32)
        zrows = zeros.shape[0]
        total = total_ref[0]
        rest = xs_hbm.shape[0] - total
        n_big = rest // zrows
        tail0 = total + n_big * zrows

        def big(j):
            return pltpu.make_async_copy(
                zeros, xs_hbm.at[pl.ds(pl.multiple_of(total + j * zrows, SUBLANES), zrows)], zsem)

        def small(j):
            return pltpu.make_async_copy(
                zeros.at[pl.ds(0, SUBLANES)],
                xs_hbm.at[pl.ds(pl.multiple_of(tail0 + j * SUBLANES, SUBLANES), SUBLANES)], zsem)

        for make, count in ((big, n_big), (small, (rest - n_big * zrows) // SUBLANES)):
            lax.fori_loop(0, count, lambda j, c, make=make: (make(j).start(), c)[1], 0)
            lax.fori_loop(0, count, lambda j, c, make=make: (make(j).wait(), c)[1], 0)


def _dispatch_call(runs, total, ldest, x2d, *, n_exp, n_rows_out, stage_rows, tb, bm):
    n_tok, d = x2d.shape
    assert n_tok % tb == 0
    return pl.pallas_call(
        functools.partial(_dispatch_kernel, n_exp=n_exp),
        out_shape=jax.ShapeDtypeStruct((n_rows_out, d), F32),
        grid_spec=pltpu.PrefetchScalarGridSpec(
            num_scalar_prefetch=5,
            grid=(n_tok // tb,),
            in_specs=[pl.BlockSpec((SUBLANES, tb), lambda i, *_: (0, i)),
                      pl.BlockSpec((tb, d), lambda i, *_: (i, 0))],
            out_specs=pl.BlockSpec(memory_space=pl.ANY),
            scratch_shapes=[pltpu.VMEM((2, stage_rows, d), F32), pltpu.VMEM((bm, d), F32),
                            pltpu.SemaphoreType.DMA((2,)), pltpu.SemaphoreType.DMA],
        ),
        compiler_params=pltpu.CompilerParams(
            dimension_semantics=("arbitrary",), vmem_limit_bytes=VMEM_LIMIT_BYTES),
        name="dispatch",
    )(*runs, total, ldest, x2d)


def _expert_kernel(blk_ref, exp_ref, lo_ref, hi_ref, first_ref, new_ref, xs_ref, wgu_ref, bgu_ref, wdn_ref, bdn_ref,
                   ys_ref, wgu_b, wdn_b):
    i = pl.program_id(0)
    bm = xs_ref.shape[0]
    f = wdn_ref.shape[0]
    lo, hi = lo_ref[i], hi_ref[i]

    @pl.when(jnp.logical_and(hi > lo, new_ref[i] == 1))
    def _():
        wgu_b[...] = wgu_ref[...].astype(BF16)
        wdn_b[...] = wdn_ref[...].astype(BF16)

    @pl.when(hi > lo)
    def _():
        gu = _dot(xs_ref[...].astype(BF16), wgu_b[...]) + bgu_ref[...]
        gate = jnp.minimum(gu[:, :f], SWIGLU_LIMIT)
        up = jnp.clip(gu[:, f:], -SWIGLU_LIMIT, SWIGLU_LIMIT)
        act = (up + 1.0) * (gate * jax.nn.sigmoid(gate * SWIGLU_ALPHA))
        y = _dot(act.astype(BF16), wdn_b[...]) + bdn_ref[...]
        row = lax.broadcasted_iota(jnp.int32, (bm, 1), 0)
        y = jnp.where(jnp.logical_and(row >= lo, row < hi), y, 0.0)

        @pl.when(first_ref[i] == 1)
        def _():
            ys_ref[...] = y

        @pl.when(first_ref[i] == 0)
        def _():
            ys_ref[...] = ys_ref[...] + y

    @pl.when(hi <= lo)
    def _():
        ys_ref[...] = jnp.zeros(ys_ref.shape, F32)


def _expert_call(l, items, xs, wgu, bgu, wdn, bdn, *, bm):
    n_rows, d = xs.shape
    f = wdn.shape[2]
    n_items = items[0].shape[0]
    assert n_rows % bm == 0
    return pl.pallas_call(
        _expert_kernel,
        out_shape=jax.ShapeDtypeStruct((n_rows, d), F32),
        grid_spec=pltpu.PrefetchScalarGridSpec(
            num_scalar_prefetch=6,
            grid=(n_items,),
            in_specs=[
                pl.BlockSpec((bm, d), lambda i, blk, ex, *_: (blk[i], 0)),
                pl.BlockSpec((None, None, d, 2 * f), lambda i, blk, ex, *_: (l, ex[i], 0, 0)),
                pl.BlockSpec((None, None, 1, 2 * f), lambda i, blk, ex, *_: (l, ex[i], 0, 0)),
                pl.BlockSpec((None, None, f, d), lambda i, blk, ex, *_: (l, ex[i], 0, 0)),
                pl.BlockSpec((None, None, 1, d), lambda i, blk, ex, *_: (l, ex[i], 0, 0)),
            ],
            out_specs=pl.BlockSpec((bm, d), lambda i, blk, ex, *_: (blk[i], 0)),
            scratch_shapes=[pltpu.VMEM((d, 2 * f), BF16), pltpu.VMEM((f, d), BF16)],
        ),
        compiler_params=pltpu.CompilerParams(
            dimension_semantics=("arbitrary",), vmem_limit_bytes=VMEM_LIMIT_BYTES),
        name="experts",
    )(*items, xs, wgu, bgu, wdn, bdn)


def _combine_kernel(nc_ref, ls_ref, gs_ref, chunks_ref, ld_ref, gt_ref, x_ref, l3g_ref, l3b_ref, ys_hbm, o_ref,
                    stage, sems, *, alpha, n_exp):
    s = pl.program_id(0)
    last = pl.num_programs(0) - 1
    tb, d = x_ref.shape
    n_rows = stage.shape[1]
    slot = s % 2

    run_chunks, tile_chunks = tb // SUBLANES, n_rows // SUBLANES

    def chunk_copy(slot_):
        def make(local_row, global_row, rows):
            return pltpu.make_async_copy(ys_hbm.at[pl.ds(global_row, rows)],
                                         stage.at[slot_, pl.ds(local_row, rows)], sems.at[slot_])
        return make

    @pl.when(s == 0)
    def _():
        stage[...] = jnp.zeros(stage.shape, F32)
        _run_copies(nc_ref, ls_ref, gs_ref, s, n_exp, run_chunks, chunk_copy(slot))

    @pl.when(s < last)
    def _():
        _run_copies(nc_ref, ls_ref, gs_ref, s + 1, n_exp, run_chunks, chunk_copy(1 - slot))

    _wait_chunks(chunks_ref[s], tile_chunks, chunk_copy(slot))

    pos = jnp.transpose(ld_ref[...].astype(F32))
    gates = jnp.transpose(gt_ref[...])
    rid = lax.broadcasted_iota(jnp.int32, (tb, n_rows), 1).astype(F32)
    weights = jnp.zeros((tb, n_rows), F32)
    for kk in range(TOP_K):
        weights = jnp.where(rid == pos[:, kk:kk + 1], gates[:, kk:kk + 1], weights)
    ff = _dot(weights.astype(BF16), stage[slot].astype(BF16))
    o_ref[...] = _layer_norm(alpha * x_ref[...] + ff, l3g_ref[...], l3b_ref[...])


def _combine_call(l, runs, ldest, gates, x2d, l3g, l3b, ys, *, alpha, n_exp, stage_rows, tb):
    n_tok, d = x2d.shape
    assert n_tok % tb == 0 and tb % LANES == 0
    slot = pl.BlockSpec((SUBLANES, tb), lambda i, *_: (0, i))
    vec = pl.BlockSpec((None, 1, d), lambda i, *_: (l, 0, 0))
    return pl.pallas_call(
        functools.partial(_combine_kernel, alpha=alpha, n_exp=n_exp),
        out_shape=jax.ShapeDtypeStruct((n_tok, d), F32),
        grid_spec=pltpu.PrefetchScalarGridSpec(
            num_scalar_prefetch=4,
            grid=(n_tok // tb,),
            in_specs=[slot, slot, pl.BlockSpec((tb, d), lambda i, *_: (i, 0)), vec, vec,
                      pl.BlockSpec(memory_space=pl.ANY)],
            out_specs=pl.BlockSpec((tb, d), lambda i, *_: (i, 0)),
            scratch_shapes=[pltpu.VMEM((2, stage_rows, d), F32), pltpu.SemaphoreType.DMA((2,))],
        ),
        compiler_params=pltpu.CompilerParams(
            dimension_semantics=("arbitrary",), vmem_limit_bytes=VMEM_LIMIT_BYTES),
        name="combine",
    )(*runs, ldest, gates, x2d, l3g, l3b, ys)


def kernel(x, mem, mem_ln_g, mem_ln_b, w_in, b_in, conv_a_w, conv_a_b, ln_a_g, ln_a_b, w_a_out, b_a_out,
           conv_b_w, w_b_out, pool_w, pool_scale, w_mix_out, ln1_g, ln1_b, w_xq, w_xk, w_xv, w_xo, ln2_g, ln2_b,
           router_w, router_b, w_gu, b_gu, w_down, b_down, ln3_g, ln3_b):
    bsz, t_len, d = x.shape
    depth = w_in.shape[0]
    n_exp = router_w.shape[2]
    n_tok = bsz * t_len
    alpha = float((2 * depth) ** 0.25)
    tile = min(512, t_len)
    rtile = min(512, t_len)
    bm = min(512, n_tok * TOP_K)
    n_tiles = n_tok // rtile
    stage_rows = _round_up(TOP_K * rtile + n_exp * (SUBLANES - 1), 2 * LANES)
    n_blocks = pl.cdiv(TOP_K * n_tok + n_tiles * n_exp * (SUBLANES - 1), bm)
    n_rows_out = (n_blocks + 1) * bm

    row = lambda a: a[:, None, :]
    bf = lambda a: a.astype(BF16)
    w_in_b, wao_b, wbo_b, pw_b, wmo_b = bf(w_in), bf(w_a_out), bf(w_b_out), bf(pool_w), bf(w_mix_out)
    wq_b, wo_b = bf(w_xq), bf(w_xo)
    rwt_b = bf(jnp.swapaxes(router_w, 1, 2))
    rb_c = router_b[:, :, None]
    bgu_r, bdn_r = b_gu[:, :, None, :], b_down[:, :, None, :]

    k_all, v_all = _kv_call(mem, mem_ln_g[None, :], mem_ln_b[None, :], bf(w_xk), bf(w_xv))

    for l in range(depth):
        x1 = _mixer_call(l, x, w_in_b, row(b_in), conv_a_w, row(conv_a_b), row(ln_a_g), row(ln_a_b), wao_b,
                         row(b_a_out), conv_b_w, wbo_b, pw_b, row(pool_scale), wmo_b, row(ln1_g), row(ln1_b),
                         alpha=alpha, tb=tile)
        x2, gates, ldest, cp = _xattn_call(l, x1, k_all, v_all, wq_b, wo_b, row(ln2_g), row(ln2_b),
                                           rwt_b, rb_c, alpha=alpha, tb=rtile)
        runs, total, items = _routing_plan(cp[:, :, 0].astype(jnp.int32), n_blocks, bm)
        x2d = x2.reshape(n_tok, d)
        xs = _dispatch_call(runs, total, ldest, x2d, n_exp=n_exp, n_rows_out=n_rows_out, stage_rows=stage_rows,
                            tb=rtile, bm=bm)
        ys = _expert_call(l, items, xs, w_gu, bgu_r, w_down, bdn_r, bm=bm)
        x = _combine_call(l, runs, ldest, gates, x2d, row(ln3_g), row(ln3_b), ys, alpha=alpha, n_exp=n_exp,
                          stage_rows=stage_rows, tb=rtile).reshape(bsz, t_len, d)
    return x
```

```python
import functools

import jax
import jax.numpy as jnp
from jax import lax
from jax.experimental import pallas as pl
from jax.experimental.pallas import tpu as pltpu

N_XHEADS = 4
TOP_K = 4
POOL_WINDOWS = (2, 4, 8, 16)
SWIGLU_LIMIT = 7.0
SWIGLU_ALPHA = 1.702
LN_EPS = 1e-5

SUBLANES = 8
LANES = 128
VMEM_LIMIT_BYTES = 56 * 1024 * 1024
CONV_ROW_CHUNK = 64

BF16 = jnp.bfloat16
F32 = jnp.float32


def _round_up(n, m):
    return (n + m - 1) // m * m


def _layer_norm(v, g, b):
    mu = jnp.mean(v, axis=-1, keepdims=True)
    c = v - mu
    var = jnp.mean(c * c, axis=-1, keepdims=True)
    return c * lax.rsqrt(var + LN_EPS) * g + b


def _dot(a, b):
    return jnp.dot(a, b, preferred_element_type=F32)


def _resident(block_shape, index_map):
    return pl.BlockSpec(block_shape, index_map, pipeline_mode=pl.Buffered(1))


def _kv_kernel(mem_ref, g_ref, b_ref, wk_ref, wv_ref, k_ref, v_ref):
    mn = _layer_norm(mem_ref[...], g_ref[...], b_ref[...]).astype(BF16)
    k_ref[...] = _dot(mn, wk_ref[...]).astype(BF16)
    v_ref[...] = _dot(mn, wv_ref[...]).astype(BF16)


def _kv_call(mem, g, b, wk, wv):
    bsz, m_len, d = mem.shape
    depth = wk.shape[0]
    out = jax.ShapeDtypeStruct((depth, bsz, m_len, d), BF16)
    return pl.pallas_call(
        _kv_kernel,
        out_shape=(out, out),
        grid=(depth, bsz),
        in_specs=[
            pl.BlockSpec((None, m_len, d), lambda l, i: (i, 0, 0)),
            pl.BlockSpec((1, d), lambda l, i: (0, 0)),
            pl.BlockSpec((1, d), lambda l, i: (0, 0)),
            pl.BlockSpec((None, d, d), lambda l, i: (l, 0, 0)),
            pl.BlockSpec((None, d, d), lambda l, i: (l, 0, 0)),
        ],
        out_specs=(
            pl.BlockSpec((None, None, m_len, d), lambda l, i: (l, i, 0, 0)),
            pl.BlockSpec((None, None, m_len, d), lambda l, i: (l, i, 0, 0)),
        ),
        compiler_params=pltpu.CompilerParams(
            dimension_semantics=("arbitrary", "arbitrary"), vmem_limit_bytes=VMEM_LIMIT_BYTES),
        name="mem_kv",
    )(mem, g, b, wk, wv)


def _causal_taps(buf, shifted, w_ref, halo, tb, row_chunk, out_ref, between=()):
    k_width = w_ref.shape[0]
    d = buf.shape[1]
    first = halo - (k_width - 1)
    n_rows = halo + tb - SUBLANES
    residues = sorted({o % SUBLANES for o in range(first, halo + 1)} - {0})
    for res in residues:
        shifted[res - 1, 0:n_rows, :] = buf[res:res + n_rows, :]
    between = list(between)
    for c0 in range(0, d, LANES):
        cols = slice(c0, c0 + LANES)
        if c0 > 0 and between:
            between.pop(0)()
        for r0 in range(0, tb, row_chunk):
            acc = None
            for o in range(first, halo + 1):
                res = o % SUBLANES
                start = r0 + o - res
                rows = buf[start:start + row_chunk, cols] if res == 0 else shifted[res - 1, start:start + row_chunk, cols]
                term = w_ref[o - first:o - first + 1, cols] * rows
                acc = term if acc is None else acc + term
            out_ref[r0:r0 + row_chunk, cols] = acc
    for job in between:
        job()


def _mixer_kernel(x_ref, w_in_ref, b_in_ref, caw_ref, cab_ref, lag_ref, lab_ref, wao_ref, bao_ref,
                  cbw_ref, wbo_ref, pw_ref, ps_ref, wmo_ref, l1g_ref, l1b_ref, o_ref,
                  abuf, bbuf, cbuf, tmp, shifted, pacc, *, alpha, halo_a, halo_b, halo_c):
    tb, d = x_ref.shape
    t = pl.program_id(1)

    @pl.when(t == 0)
    def _():
        abuf[0:halo_a, :] = jnp.zeros((halo_a, d), F32)
        bbuf[0:halo_b, :] = jnp.zeros((halo_b, d), F32)
        cbuf[0:halo_c, :] = jnp.zeros((halo_c, d), F32)
        pacc[0:SUBLANES, :] = jnp.zeros((SUBLANES, d), F32)

    x = x_ref[...]
    xb = x.astype(BF16)

    def proj(j):
        return _dot(xb, w_in_ref[:, j * d:(j + 1) * d]) + b_in_ref[:, j * d:(j + 1) * d]

    held = {}

    def stage_b():
        bbuf[halo_b:halo_b + tb, :] = proj(3) * proj(4)

    def stage_c():
        cbuf[halo_c:halo_c + tb, :] = proj(5)

    def hold(name, j, fn):
        def job():
            held[name] = fn(proj(j))
        return job

    jobs = [stage_b, hold("gate_b", 2, lambda v: v), stage_c, hold("mix_a", 6, jax.nn.sigmoid),
            hold("mix_b", 7, jax.nn.sigmoid), hold("mix_c", 8, jax.nn.sigmoid)]

    abuf[halo_a:halo_a + tb, :] = proj(0) * jax.nn.sigmoid(proj(1))
    _causal_taps(abuf, shifted, caw_ref, halo_a, tb, CONV_ROW_CHUNK, tmp, between=jobs)
    a = _layer_norm(tmp[...] + cab_ref[...], lag_ref[...], lab_ref[...])
    a = a * jax.nn.sigmoid(a)
    y_a = _dot(a.astype(BF16), wao_ref[...]) + bao_ref[...]
    m = held["mix_a"] * y_a
    abuf[0:halo_a, :] = abuf[tb:tb + halo_a, :]

    kb = cbw_ref.shape[0]
    u_b = None
    for k in range(kb):
        term = cbw_ref[k:k + 1, :] * bbuf[halo_b - (kb - 1) + k:halo_b - (kb - 1) + k + tb, :]
        u_b = term if u_b is None else u_b + term
    y_b = _dot((held["gate_b"] * u_b).astype(BF16), wbo_ref[...])
    m = m + held["mix_b"] * y_b
    bbuf[0:halo_b, :] = bbuf[tb:tb + halo_b, :]

    n_groups = len(POOL_WINDOWS)
    pg = d // n_groups
    pos = t * tb + lax.broadcasted_iota(jnp.int32, (tb, 1), 0)
    assert all(w == 2 << i for i, w in enumerate(POOL_WINDOWS))
    ext = halo_c + tb
    y_parts = []
    sums = cbuf
    for gi, win in enumerate(POOL_WINDOWS):
        lo, k = gi * pg, win // 2
        pacc[k:ext, lo:d] = sums[k:ext, lo:d] + sums[0:ext - k, lo:d]
        sums = pacc
        u = cbuf[halo_c:halo_c + tb, lo:lo + pg]
        s = pacc[halo_c:halo_c + tb, lo:lo + pg]
        cnt = jnp.minimum(pos + 1, win).astype(F32)
        pooled = s / cnt - u
        y_parts.append(_dot(pooled.astype(BF16), pw_ref[gi]))
    y_c = jnp.concatenate(y_parts, axis=-1) * ps_ref[...]
    m = m + held["mix_c"] * y_c
    cbuf[0:halo_c, :] = cbuf[tb:tb + halo_c, :]

    mix = _dot(m.astype(BF16), wmo_ref[...])
    o_ref[...] = _layer_norm(alpha * x + mix, l1g_ref[...], l1b_ref[...])


def _mixer_call(l, x, w_in, b_in, caw, cab, lag, lab, wao, bao, cbw, wbo, pw, ps, wmo, l1g, l1b, *, alpha, tb):
    bsz, t_len, d = x.shape
    d_in = w_in.shape[2]
    ka, kb = caw.shape[1], cbw.shape[1]
    n_groups, pg = pw.shape[1], pw.shape[2]
    halo_a = _round_up(ka - 1, SUBLANES)
    halo_b = _round_up(kb - 1, SUBLANES)
    halo_c = _round_up(max(POOL_WINDOWS) - 1, SUBLANES)
    assert t_len % tb == 0 and tb % LANES == 0 and tb >= halo_a
    vec = lambda n: pl.BlockSpec((None, 1, n), lambda i, t: (l, 0, 0))
    kern = functools.partial(_mixer_kernel, alpha=alpha, halo_a=halo_a, halo_b=halo_b, halo_c=halo_c)
    return pl.pallas_call(
        kern,
        out_shape=jax.ShapeDtypeStruct((bsz, t_len, d), F32),
        grid=(bsz, t_len // tb),
        in_specs=[
            pl.BlockSpec((None, tb, d), lambda i, t: (i, t, 0)),
            _resident((None, d, d_in), lambda i, t: (l, 0, 0)),
            vec(d_in),
            pl.BlockSpec((None, ka, d), lambda i, t: (l, 0, 0)),
            vec(d), vec(d), vec(d),
            _resident((None, d, d), lambda i, t: (l, 0, 0)),
            vec(d),
            pl.BlockSpec((None, kb, d), lambda i, t: (l, 0, 0)),
            _resident((None, d, d), lambda i, t: (l, 0, 0)),
            _resident((None, n_groups, pg, pg), lambda i, t: (l, 0, 0, 0)),
            vec(d),
            _resident((None, d, d), lambda i, t: (l, 0, 0)),
            vec(d), vec(d),
        ],
        out_specs=pl.BlockSpec((None, tb, d), lambda i, t: (i, t, 0)),
        scratch_shapes=[
            pltpu.VMEM((halo_a + tb, d), F32),
            pltpu.VMEM((halo_b + tb, d), F32),
            pltpu.VMEM((halo_c + tb, d), F32),
            pltpu.VMEM((tb, d), F32),
            pltpu.VMEM((SUBLANES - 1, halo_a + tb - SUBLANES, d), F32),
            pltpu.VMEM((halo_c + tb, d), F32),
        ],
        compiler_params=pltpu.CompilerParams(
            dimension_semantics=("arbitrary", "arbitrary"), vmem_limit_bytes=VMEM_LIMIT_BYTES),
        name="mixer",
    )(x, w_in, b_in, caw, cab, lag, lab, wao, bao, cbw, wbo, pw, ps, wmo, l1g, l1b)


def _xattn_kernel(x_ref, k_ref, v_ref, wq_ref, wo_ref, l2g_ref, l2b_ref, rwt_ref, rb_ref,
                  o_ref, gt_ref, ld_ref, cp_ref, *, alpha):
    tb, d = x_ref.shape
    n_exp = rwt_ref.shape[0]
    dh = d // N_XHEADS
    scale = dh ** -0.5

    x = x_ref[...]
    q = _dot(x.astype(BF16), wq_ref[...])
    heads = []
    for h in range(N_XHEADS):
        qh = q[:, h * dh:(h + 1) * dh].astype(BF16)
        kh = k_ref[:, h * dh:(h + 1) * dh]
        vh = v_ref[:, h * dh:(h + 1) * dh]
        s = lax.dot_general(qh, kh, (((1,), (1,)), ((), ())), preferred_element_type=F32) * scale
        p = jnp.exp(s - jnp.max(s, axis=-1, keepdims=True))
        denom = jnp.sum(p, axis=-1, keepdims=True)
        heads.append(_dot(p.astype(BF16), vh) / denom)
    att = _dot(jnp.concatenate(heads, axis=-1).astype(BF16), wo_ref[...])
    x2 = _layer_norm(alpha * x + att, l2g_ref[...], l2b_ref[...])
    o_ref[...] = x2

    logits = lax.dot_general(rwt_ref[...], x2.astype(BF16), (((1,), (1,)), ((), ())),
                             preferred_element_type=F32) + rb_ref[...]
    eidx = lax.broadcasted_iota(jnp.int32, (n_exp, tb), 0).astype(F32)
    work = logits
    tops, args = [], []
    member = jnp.zeros((n_exp, tb), F32)
    for _ in range(TOP_K):
        mx = jnp.max(work, axis=0, keepdims=True)
        am = jnp.min(jnp.where(work == mx, eidx, float(n_exp)), axis=0, keepdims=True)
        hit = eidx == am
        tops.append(mx)
        args.append(am)
        member = jnp.where(hit, 1.0, member)
        work = jnp.where(hit, -jnp.inf, work)
    exps = [jnp.exp(tv - tops[0]) for tv in tops]
    esum = exps[0]
    for ev in exps[1:]:
        esum = esum + ev

    before = lax.broadcasted_iota(jnp.int32, (tb, tb), 0) < lax.broadcasted_iota(jnp.int32, (tb, tb), 1)
    prefix = _dot(member.astype(BF16), jnp.where(before, 1.0, 0.0).astype(BF16))
    count = jnp.sum(member, axis=1, keepdims=True)
    padded = jnp.floor((count + (SUBLANES - 1)) * (1.0 / SUBLANES)) * SUBLANES
    padded = jnp.broadcast_to(padded, (n_exp, LANES))
    lower = lax.broadcasted_iota(jnp.int32, (n_exp, n_exp), 1) < lax.broadcasted_iota(jnp.int32, (n_exp, n_exp), 0)
    run_start = _dot(jnp.where(lower, 1.0, 0.0).astype(BF16), padded.astype(BF16))
    position = prefix + run_start[:, 0:1]
    gt_ref[...] = jnp.zeros(gt_ref.shape, F32)
    ld_ref[...] = jnp.zeros(ld_ref.shape, jnp.int32)
    for kk in range(TOP_K):
        gt_ref[kk:kk + 1, :] = exps[kk] / esum
        pos = jnp.sum(jnp.where(eidx == args[kk], position, 0.0), axis=0, keepdims=True)
        ld_ref[kk:kk + 1, :] = pos.astype(jnp.int32)
    cp_ref[...] = padded


def _xattn_call(l, x, k_all, v_all, wq, wo, l2g, l2b, rwt, rb, *, alpha, tb):
    bsz, t_len, d = x.shape
    m_len = k_all.shape[2]
    n_exp = rwt.shape[1]
    n_tok = bsz * t_len
    nt = t_len // tb
    assert t_len % tb == 0 and tb % LANES == 0
    vec = lambda n: pl.BlockSpec((None, 1, n), lambda i, t: (l, 0, 0))
    slot = pl.BlockSpec((SUBLANES, tb), lambda i, t: (0, i * nt + t))
    return pl.pallas_call(
        functools.partial(_xattn_kernel, alpha=alpha),
        out_shape=(
            jax.ShapeDtypeStruct((bsz, t_len, d), F32),
            jax.ShapeDtypeStruct((SUBLANES, n_tok), F32),
            jax.ShapeDtypeStruct((SUBLANES, n_tok), jnp.int32),
            jax.ShapeDtypeStruct((n_tok // tb, n_exp, LANES), F32),
        ),
        grid=(bsz, nt),
        in_specs=[
            pl.BlockSpec((None, tb, d), lambda i, t: (i, t, 0)),
            pl.BlockSpec((None, None, m_len, d), lambda i, t: (l, i, 0, 0)),
            pl.BlockSpec((None, None, m_len, d), lambda i, t: (l, i, 0, 0)),
            _resident((None, d, d), lambda i, t: (l, 0, 0)),
            _resident((None, d, d), lambda i, t: (l, 0, 0)),
            vec(d), vec(d),
            pl.BlockSpec((None, n_exp, d), lambda i, t: (l, 0, 0)),
            pl.BlockSpec((None, n_exp, 1), lambda i, t: (l, 0, 0)),
        ],
        out_specs=(
            pl.BlockSpec((None, tb, d), lambda i, t: (i, t, 0)),
            slot, slot,
            pl.BlockSpec((None, n_exp, LANES), lambda i, t: (i * nt + t, 0, 0)),
        ),
        compiler_params=pltpu.CompilerParams(
            dimension_semantics=("arbitrary", "arbitrary"), vmem_limit_bytes=VMEM_LIMIT_BYTES),
        name="xattn_router",
    )(x, k_all, v_all, wq, wo, l2g, l2b, rwt, rb)


def _routing_plan(cp, n_blocks, bm):
    n_exp = cp.shape[1]
    local_start = jnp.cumsum(cp, axis=1) - cp
    tot = jnp.sum(cp, axis=0)
    gend = jnp.cumsum(tot)
    gstart = gend - tot
    global_start = gstart[None, :] + jnp.cumsum(cp, axis=0) - cp
    total = gend[-1]
    tile_chunks = jnp.sum(cp, axis=1) // SUBLANES

    n_items = n_blocks + n_exp + 1
    first_blk = gstart // bm
    nblk = jnp.where(tot > 0, (gend + bm - 1) // bm - first_blk, 0)
    item_end = jnp.cumsum(nblk)
    item_start = item_end - nblk
    it = jnp.arange(n_items, dtype=jnp.int32)
    e_raw = jnp.sum(item_end[None, :] <= it[:, None], axis=1).astype(jnp.int32)
    valid = e_raw < n_exp
    e = jnp.minimum(e_raw, n_exp - 1)
    unused_blk = jnp.minimum((total - 1) // bm + 1 + it - item_end[-1], n_blocks)
    is_e = e[:, None] == jnp.arange(n_exp, dtype=jnp.int32)[None, :]
    of_e = lambda table: jnp.sum(jnp.where(is_e, table[None, :], 0), axis=1)
    blk = jnp.where(valid, of_e(first_blk) + it - of_e(item_start), unused_blk)
    lo = jnp.where(valid, jnp.maximum(of_e(gstart), blk * bm) - blk * bm, 0)
    hi = jnp.where(valid, jnp.minimum(of_e(gend), (blk + 1) * bm) - blk * bm, 0)
    first = jnp.logical_or(jnp.logical_not(valid), lo == 0)
    new_expert = jnp.concatenate([jnp.ones((1,), bool), e[1:] != e[:-1]])
    i32 = lambda a: a.astype(jnp.int32)
    runs = tuple(i32(a.reshape(-1)) for a in (cp // SUBLANES, local_start, global_start)) + (i32(tile_chunks),)
    return runs, i32(total.reshape(1)), tuple(map(i32, (blk, e, lo, hi, first, new_expert)))


def _run_copies(nc_ref, ls_ref, gs_ref, step, n_exp, max_chunks, make_copy):
    def per_expert(e, carry):
        idx = step * n_exp + e
        n, local0, global0 = nc_ref[idx], ls_ref[idx], gs_ref[idx]
        for b in range(max_chunks.bit_length()):
            @pl.when(((n >> b) & 1) == 1)
            def _():
                off = ((n >> (b + 1)) << (b + 1)) * SUBLANES
                make_copy(pl.multiple_of(local0 + off, SUBLANES), pl.multiple_of(global0 + off, SUBLANES),
                          SUBLANES << b).start()
        return carry

    lax.fori_loop(0, n_exp, per_expert, 0)


def _wait_chunks(n_chunks, max_chunks, make_copy):
    for b in range(max_chunks.bit_length()):
        @pl.when(((n_chunks >> b) & 1) == 1)
        def _():
            make_copy(0, 0, SUBLANES << b).wait()


def _dispatch_kernel(nc_ref, ls_ref, gs_ref, chunks_ref, total_ref, ld_ref, x_ref, xs_hbm, stage, zeros, sems, zsem,
                     *, n_exp):
    s = pl.program_id(0)
    last = pl.num_programs(0) - 1
    tb, d = x_ref.shape
    n_rows = stage.shape[1]
    slot = s % 2

    run_chunks, tile_chunks = tb // SUBLANES, n_rows // SUBLANES

    def chunk_copy(slot_):
        def make(local_row, global_row, rows):
            return pltpu.make_async_copy(stage.at[slot_, pl.ds(local_row, rows)],
                                         xs_hbm.at[pl.ds(global_row, rows)], sems.at[slot_])
        return make

    rid = lax.broadcasted_iota(jnp.int32, (n_rows, tb), 0)
    onehot = jnp.zeros((n_rows, tb), F32)
    for kk in range(TOP_K):
        onehot = jnp.where(rid == ld_ref[kk:kk + 1, :], 1.0, onehot)
    stage[slot] = _dot(onehot.astype(BF16), x_ref[...].astype(BF16))

    _run_copies(nc_ref, ls_ref, gs_ref, s, n_exp, run_chunks, chunk_copy(slot))

    @pl.when(s > 0)
    def _():
        _wait_chunks(chunks_ref[s - 1], tile_chunks, chunk_copy(1 - slot))

    @pl.when(s == last)
    def _():
        _wait_chunks(chunks_ref[s], tile_chunks, chunk_copy(slot))
        zeros[...] = jnp.zeros(zeros.shape, F32)
        zrows = zeros.shape[0]
        total = total_ref[0]
        rest = xs_hbm.shape[0] - total
        n_big = rest // zrows
        tail0 = total + n_big * zrows

        def big(j):
            return pltpu.make_async_copy(
                zeros, xs_hbm.at[pl.ds(pl.multiple_of(total + j * zrows, SUBLANES), zrows)], zsem)

        def small(j):
            return pltpu.make_async_copy(
                zeros.at[pl.ds(0, SUBLANES)],
                xs_hbm.at[pl.ds(pl.multiple_of(tail0 + j * SUBLANES, SUBLANES), SUBLANES)], zsem)

        for make, count in ((big, n_big), (small, (rest - n_big * zrows) // SUBLANES)):
            lax.fori_loop(0, count, lambda j, c, make=make: (make(j).start(), c)[1], 0)
            lax.fori_loop(0, count, lambda j, c, make=make: (make(j).wait(), c)[1], 0)


def _dispatch_call(runs, total, ldest, x2d, *, n_exp, n_rows_out, stage_rows, tb, bm):
    n_tok, d = x2d.shape
    assert n_tok % tb == 0
    return pl.pallas_call(
        functools.partial(_dispatch_kernel, n_exp=n_exp),
        out_shape=jax.ShapeDtypeStruct((n_rows_out, d), F32),
        grid_spec=pltpu.PrefetchScalarGridSpec(
            num_scalar_prefetch=5,
            grid=(n_tok // tb,),
            in_specs=[pl.BlockSpec((SUBLANES, tb), lambda i, *_: (0, i)),
                      pl.BlockSpec((tb, d), lambda i, *_: (i, 0))],
            out_specs=pl.BlockSpec(memory_space=pl.ANY),
            scratch_shapes=[pltpu.VMEM((2, stage_rows, d), F32), pltpu.VMEM((bm, d), F32),
                            pltpu.SemaphoreType.DMA((2,)), pltpu.SemaphoreType.DMA],
        ),
        compiler_params=pltpu.CompilerParams(
            dimension_semantics=("arbitrary",), vmem_limit_bytes=VMEM_LIMIT_BYTES),
        name="dispatch",
    )(*runs, total, ldest, x2d)


def _expert_kernel(blk_ref, exp_ref, lo_ref, hi_ref, first_ref, new_ref, xs_ref, wgu_ref, bgu_ref, wdn_ref, bdn_ref,
                   ys_ref, wgu_b, wdn_b):
    i = pl.program_id(0)
    bm = xs_ref.shape[0]
    f = wdn_ref.shape[0]
    lo, hi = lo_ref[i], hi_ref[i]

    @pl.when(jnp.logical_and(hi > lo, new_ref[i] == 1))
    def _():
        wgu_b[...] = wgu_ref[...].astype(BF16)
        wdn_b[...] = wdn_ref[...].astype(BF16)

    @pl.when(hi > lo)
    def _():
        gu = _dot(xs_ref[...].astype(BF16), wgu_b[...]) + bgu_ref[...]
        gate = jnp.minimum(gu[:, :f], SWIGLU_LIMIT)
        up = jnp.clip(gu[:, f:], -SWIGLU_LIMIT, SWIGLU_LIMIT)
        act = (up + 1.0) * (gate * jax.nn.sigmoid(gate * SWIGLU_ALPHA))
        y = _dot(act.astype(BF16), wdn_b[...]) + bdn_ref[...]
        row = lax.broadcasted_iota(jnp.int32, (bm, 1), 0)
        y = jnp.where(jnp.logical_and(row >= lo, row < hi), y, 0.0)

        @pl.when(first_ref[i] == 1)
        def _():
            ys_ref[...] = y

        @pl.when(first_ref[i] == 0)
        def _():
            ys_ref[...] = ys_ref[...] + y

    @pl.when(hi <= lo)
    def _():
        ys_ref[...] = jnp.zeros(ys_ref.shape, F32)


def _expert_call(l, items, xs, wgu, bgu, wdn, bdn, *, bm):
    n_rows, d = xs.shape
    f = wdn.shape[2]
    n_items = items[0].shape[0]
    assert n_rows % bm == 0
    return pl.pallas_call(
        _expert_kernel,
        out_shape=jax.ShapeDtypeStruct((n_rows, d), F32),
        grid_spec=pltpu.PrefetchScalarGridSpec(
            num_scalar_prefetch=6,
            grid=(n_items,),
            in_specs=[
                pl.BlockSpec((bm, d), lambda i, blk, ex, *_: (blk[i], 0)),
                pl.BlockSpec((None, None, d, 2 * f), lambda i, blk, ex, *_: (l, ex[i], 0, 0)),
                pl.BlockSpec((None, None, 1, 2 * f), lambda i, blk, ex, *_: (l, ex[i], 0, 0)),
                pl.BlockSpec((None, None, f, d), lambda i, blk, ex, *_: (l, ex[i], 0, 0)),
                pl.BlockSpec((None, None, 1, d), lambda i, blk, ex, *_: (l, ex[i], 0, 0)),
            ],
            out_specs=pl.BlockSpec((bm, d), lambda i, blk, ex, *_: (blk[i], 0)),
            scratch_shapes=[pltpu.VMEM((d, 2 * f), BF16), pltpu.VMEM((f, d), BF16)],
        ),
        compiler_params=pltpu.CompilerParams(
            dimension_semantics=("arbitrary",), vmem_limit_bytes=VMEM_LIMIT_BYTES),
        name="experts",
    )(*items, xs, wgu, bgu, wdn, bdn)


def _combine_kernel(nc_ref, ls_ref, gs_ref, chunks_ref, ld_ref, gt_ref, x_ref, l3g_ref, l3b_ref, ys_hbm, o_ref,
                    stage, sems, *, alpha, n_exp):
    s = pl.program_id(0)
    last = pl.num_programs(0) - 1
    tb, d = x_ref.shape
    n_rows = stage.shape[1]
    slot = s % 2

    run_chunks, tile_chunks = tb // SUBLANES, n_rows // SUBLANES

    def chunk_copy(slot_):
        def make(local_row, global_row, rows):
            return pltpu.make_async_copy(ys_hbm.at[pl.ds(global_row, rows)],
                                         stage.at[slot_, pl.ds(local_row, rows)], sems.at[slot_])
        return make

    @pl.when(s == 0)
    def _():
        stage[...] = jnp.zeros(stage.shape, F32)
        _run_copies(nc_ref, ls_ref, gs_ref, s, n_exp, run_chunks, chunk_copy(slot))

    @pl.when(s < last)
    def _():
        _run_copies(nc_ref, ls_ref, gs_ref, s + 1, n_exp, run_chunks, chunk_copy(1 - slot))

    _wait_chunks(chunks_ref[s], tile_chunks, chunk_copy(slot))

    pos = jnp.transpose(ld_ref[...].astype(F32))
    gates = jnp.transpose(gt_ref[...])
    rid = lax.broadcasted_iota(jnp.int32, (tb, n_rows), 1).astype(F32)
    weights = jnp.zeros((tb, n_rows), F32)
    for kk in range(TOP_K):
        weights = jnp.where(rid == pos[:, kk:kk + 1], gates[:, kk:kk + 1], weights)
    ff = _dot(weights.astype(BF16), stage[slot].astype(BF16))
    o_ref[...] = _layer_norm(alpha * x_ref[...] + ff, l3g_ref[...], l3b_ref[...])


def _combine_call(l, runs, ldest, gates, x2d, l3g, l3b, ys, *, alpha, n_exp, stage_rows, tb):
    n_tok, d = x2d.shape
    assert n_tok % tb == 0 and tb % LANES == 0
    slot = pl.BlockSpec((SUBLANES, tb), lambda i, *_: (0, i))
    vec = pl.BlockSpec((None, 1, d), lambda i, *_: (l, 0, 0))
    return pl.pallas_call(
        functools.partial(_combine_kernel, alpha=alpha, n_exp=n_exp),
        out_shape=jax.ShapeDtypeStruct((n_tok, d), F32),
        grid_spec=pltpu.PrefetchScalarGridSpec(
            num_scalar_prefetch=4,
            grid=(n_tok // tb,),
            in_specs=[slot, slot, pl.BlockSpec((tb, d), lambda i, *_: (i, 0)), vec, vec,
                      pl.BlockSpec(memory_space=pl.ANY)],
            out_specs=pl.BlockSpec((tb, d), lambda i, *_: (i, 0)),
            scratch_shapes=[pltpu.VMEM((2, stage_rows, d), F32), pltpu.SemaphoreType.DMA((2,))],
        ),
        compiler_params=pltpu.CompilerParams(
            dimension_semantics=("arbitrary",), vmem_limit_bytes=VMEM_LIMIT_BYTES),
        name="combine",
    )(*runs, ldest, gates, x2d, l3g, l3b, ys)


def kernel(x, mem, mem_ln_g, mem_ln_b, w_in, b_in, conv_a_w, conv_a_b, ln_a_g, ln_a_b, w_a_out, b_a_out,
           conv_b_w, w_b_out, pool_w, pool_scale, w_mix_out, ln1_g, ln1_b, w_xq, w_xk, w_xv, w_xo, ln2_g, ln2_b,
           router_w, router_b, w_gu, b_gu, w_down, b_down, ln3_g, ln3_b):
    bsz, t_len, d = x.shape
    depth = w_in.shape[0]
    n_exp = router_w.shape[2]
    n_tok = bsz * t_len
    alpha = float((2 * depth) ** 0.25)
    tile = min(512, t_len)
    rtile = min(512, t_len)
    bm = min(512, n_tok * TOP_K)
    n_tiles = n_tok // rtile
    stage_rows = _round_up(TOP_K * rtile + n_exp * (SUBLANES - 1), 2 * LANES)
    n_blocks = pl.cdiv(TOP_K * n_tok + n_tiles * n_exp * (SUBLANES - 1), bm)
    n_rows_out = (n_blocks + 1) * bm

    row = lambda a: a[:, None, :]
    bf = lambda a: a.astype(BF16)
    w_in_b, wao_b, wbo_b, pw_b, wmo_b = bf(w_in), bf(w_a_out), bf(w_b_out), bf(pool_w), bf(w_mix_out)
    wq_b, wo_b = bf(w_xq), bf(w_xo)
    rwt_b = bf(jnp.swapaxes(router_w, 1, 2))
    rb_c = router_b[:, :, None]
    bgu_r, bdn_r = b_gu[:, :, None, :], b_down[:, :, None, :]

    k_all, v_all = _kv_call(mem, mem_ln_g[None, :], mem_ln_b[None, :], bf(w_xk), bf(w_xv))

    for l in range(depth):
        x1 = _mixer_call(l, x, w_in_b, row(b_in), conv_a_w, row(conv_a_b), row(ln_a_g), row(ln_a_b), wao_b,
                         row(b_a_out), conv_b_w, wbo_b, pw_b, row(pool_scale), wmo_b, row(ln1_g), row(ln1_b),
                         alpha=alpha, tb=tile)
        x2, gates, ldest, cp = _xattn_call(l, x1, k_all, v_all, wq_b, wo_b, row(ln2_g), row(ln2_b),
                                           rwt_b, rb_c, alpha=alpha, tb=rtile)
        runs, total, items = _routing_plan(cp[:, :, 0].astype(jnp.int32), n_blocks, bm)
        x2d = x2.reshape(n_tok, d)
        xs = _dispatch_call(runs, total, ldest, x2d, n_exp=n_exp, n_rows_out=n_rows_out, stage_rows=stage_rows,
                            tb=rtile, bm=bm)
        ys = _expert_call(l, items, xs, w_gu, bgu_r, w_down, bdn_r, bm=bm)
        x = _combine_call(l, runs, ldest, gates, x2d, row(ln3_g), row(ln3_b), ys, alpha=alpha, n_exp=n_exp,
                          stage_rows=stage_rows, tb=rtile).reshape(bsz, t_len, d)
    return x
```
